```python
import jax
import jax.numpy as jnp
from jax import lax
import numpy as np

D_MODEL = 2048
BATCH = 16
SEQ = 2048
DEPTH = 4

BRANCH_WIDTH = D_MODEL // 2
N_BRANCH = 3
HEAD_DIM = 64
N_Q_HEADS = BRANCH_WIDTH // HEAD_DIM
N_KV_HEADS = 4
GQA_GROUP = N_Q_HEADS // N_KV_HEADS
WINDOW = 128
BLOCK = 128
ATTN_WIDTH = N_Q_HEADS * HEAD_DIM
KV_WIDTH = N_KV_HEADS * HEAD_DIM
CONV_WIDTH = BRANCH_WIDTH
CONV_KERNEL = 31
N_MEM = 256
MEM_HEADS = 4
MEM_HEAD_DIM = BRANCH_WIDTH // MEM_HEADS
MEM_WIDTH = MEM_HEADS * MEM_HEAD_DIM
IN_SPLITS = (ATTN_WIDTH,
             ATTN_WIDTH + KV_WIDTH,
             ATTN_WIDTH + 2 * KV_WIDTH,
             ATTN_WIDTH + 2 * KV_WIDTH + 2 * CONV_WIDTH,
             ATTN_WIDTH + 2 * KV_WIDTH + 2 * CONV_WIDTH + MEM_WIDTH)
IN_WIDTH = IN_SPLITS[-1] + N_BRANCH * D_MODEL
N_GROUPS = 4
EXPERTS_PER_GROUP = 8
N_EXPERTS = N_GROUPS * EXPERTS_PER_GROUP
TOP_K_IN_GROUP = 2
D_EXPERT = D_MODEL // 4
ALPHA = (2.0 * DEPTH) ** 0.25
BETA = (8.0 * DEPTH) ** -0.25
LN_EPS = 1e-5
NEG_INF = -1e30

kernel_name = "hybrid_swa_conformer_mem_hmoe_deepnorm"


def layer_norm(x, g, b):
    x32 = x.astype(jnp.float32)
    mu = jnp.mean(x32, axis=-1, keepdims=True)
    var = jnp.mean(jnp.square(x32 - mu), axis=-1, keepdims=True)
    return ((x32 - mu) * lax.rsqrt(var + LN_EPS) * g + b).astype(x.dtype)


def alibi_slopes(n_heads):
    h = jnp.arange(1, n_heads + 1, dtype=jnp.float32)
    return jnp.exp2(-8.0 * h / n_heads)


def sliding_window_attention(q, k, v, sinks):
    b, s = q.shape[0], q.shape[1]
    nb = s // BLOCK
    qb = q.reshape(b, nb, BLOCK, N_KV_HEADS, GQA_GROUP, HEAD_DIM).astype(jnp.float32)

    def with_prev(t):
        tb = t.reshape(b, nb, BLOCK, N_KV_HEADS, HEAD_DIM).astype(jnp.float32)
        prev = jnp.pad(tb[:, :-1], ((0, 0), (1, 0), (0, 0), (0, 0), (0, 0)))
        return jnp.concatenate([prev, tb], axis=2)

    kb, vb = with_prev(k), with_prev(v)
    scores = jnp.einsum('bnqkgd,bnskd->bnkgqs', qb, kb) * (HEAD_DIM ** -0.5)
    qi = jnp.arange(BLOCK)[:, None] + BLOCK
    si = jnp.arange(2 * BLOCK)[None, :]
    dist = qi - si
    key_pos = jnp.arange(nb)[:, None, None] * BLOCK - BLOCK + si[None]
    valid = (dist >= 0)[None] & (dist < WINDOW)[None] & (key_pos >= 0)
    slopes = alibi_slopes(N_Q_HEADS).reshape(N_KV_HEADS, GQA_GROUP)
    bias = -slopes[:, :, None, None] * dist.astype(jnp.float32)[None, None]
    scores = jnp.where(valid[None, :, None, None], scores + bias[None, None], NEG_INF)
    sink = sinks.astype(jnp.float32).reshape(1, 1, N_KV_HEADS, GQA_GROUP, 1, 1)
    m = jnp.maximum(jnp.max(scores, axis=-1, keepdims=True), sink)
    p = jnp.exp(scores - m)
    probs = p / (jnp.sum(p, axis=-1, keepdims=True) + jnp.exp(sink - m))
    out = jnp.einsum('bnkgqs,bnskd->bnqkgd', probs, vb)
    return out.reshape(b, s, ATTN_WIDTH).astype(q.dtype)


def conformer_conv(u, w_dw, b_dw, ln_g, ln_b):
    a, gate = jnp.split(u, 2, axis=-1)
    h = a * jax.nn.sigmoid(gate)
    h = lax.conv_general_dilated(
        h, w_dw[:, None, :].astype(h.dtype), window_strides=(1,),
        padding=((CONV_KERNEL - 1, 0),),
        dimension_numbers=('NWC', 'WIO', 'NWC'),
        feature_group_count=CONV_WIDTH) + b_dw
    h = layer_norm(h, ln_g, ln_b)
    return jax.nn.silu(h)


def memory_attention(qm, mem_n, w_kv):
    b, s = qm.shape[0], qm.shape[1]
    km, vm = jnp.split(mem_n @ w_kv, 2, axis=-1)
    km = km.reshape(b, N_MEM, MEM_HEADS, MEM_HEAD_DIM).astype(jnp.float32)
    vm = vm.reshape(b, N_MEM, MEM_HEADS, MEM_HEAD_DIM).astype(jnp.float32)
    qh = qm.reshape(b, s, MEM_HEADS, MEM_HEAD_DIM).astype(jnp.float32)
    sc = jnp.einsum('bshd,bmhd->bhsm', qh, km) * (MEM_HEAD_DIM ** -0.5)
    p = jax.nn.softmax(sc, axis=-1)
    out = jnp.einsum('bhsm,bmhd->bshd', p, vm)
    return out.reshape(b, s, MEM_WIDTH).astype(qm.dtype)


def hierarchical_moe(x, wg, bg, we, be, w1, w3, w2):
    b, s, d = x.shape
    t = x.reshape(b * s, d)
    g_logits = (t @ wg).astype(jnp.float32) + bg
    g_prob = jax.nn.softmax(g_logits, axis=-1)
    _, grp = lax.top_k(g_logits, 1)
    p_grp = jnp.take_along_axis(g_prob, grp, axis=-1)
    e_logits = ((t @ we).astype(jnp.float32) + be).reshape(-1, N_GROUPS, EXPERTS_PER_GROUP)
    e_in = jnp.take_along_axis(e_logits, grp[:, :, None], axis=1)[:, 0]
    top_v, top_i = lax.top_k(e_in, TOP_K_IN_GROUP)
    w_top = jax.nn.softmax(top_v, axis=-1) * p_grp
    eidx = grp * EXPERTS_PER_GROUP + top_i
    comb = jnp.sum(jax.nn.one_hot(eidx, N_EXPERTS, dtype=jnp.float32) * w_top[..., None], axis=1)
    out = jnp.zeros(t.shape, jnp.float32)
    for e in range(N_EXPERTS):
        h = jax.nn.silu(t @ w1[e]) * (t @ w3[e])
        out = out + comb[:, e:e + 1] * (h @ w2[e])
    return out.reshape(b, s, d).astype(x.dtype)


def setup_inputs(seed: int = 0) -> dict:
    key = jax.random.key(seed)
    ks = jax.random.split(key, 24)
    L = DEPTH

    def nrm(k, shape, scale):
        return jax.random.normal(k, shape, jnp.float32) * scale

    return {
        'x': nrm(ks[0], (BATCH, SEQ, D_MODEL), 1.0),
        'mem': nrm(ks[1], (BATCH, N_MEM, D_MODEL), 1.0),
        'mem_ln_g': 1.0 + nrm(ks[2], (D_MODEL,), 0.02),
        'mem_ln_b': nrm(ks[3], (D_MODEL,), 0.02),
        'w_in': nrm(ks[4], (L, D_MODEL, IN_WIDTH), D_MODEL ** -0.5),
        'attn_sinks': nrm(ks[5], (L, N_Q_HEADS), 0.5),
        'conv_dw': nrm(ks[6], (L, CONV_KERNEL, CONV_WIDTH), CONV_KERNEL ** -0.5),
        'conv_dw_b': nrm(ks[7], (L, CONV_WIDTH), 0.02),
        'conv_ln_g': 1.0 + nrm(ks[8], (L, CONV_WIDTH), 0.02),
        'conv_ln_b': nrm(ks[9], (L, CONV_WIDTH), 0.02),
        'w_mem_kv': nrm(ks[10], (L, D_MODEL, 2 * MEM_WIDTH), D_MODEL ** -0.5),
        'w_branch': nrm(ks[11], (L, N_BRANCH, BRANCH_WIDTH, D_MODEL), BRANCH_WIDTH ** -0.5),
        'w_out': nrm(ks[12], (L, D_MODEL, D_MODEL), BETA * D_MODEL ** -0.5),
        'ln1_g': 1.0 + nrm(ks[13], (L, D_MODEL), 0.02),
        'ln1_b': nrm(ks[14], (L, D_MODEL), 0.02),
        'router_group': nrm(ks[15], (L, D_MODEL, N_GROUPS), D_MODEL ** -0.5),
        'router_group_b': nrm(ks[16], (L, N_GROUPS), 0.01),
        'router_expert': nrm(ks[17], (L, D_MODEL, N_EXPERTS), D_MODEL ** -0.5),
        'router_expert_b': nrm(ks[18], (L, N_EXPERTS), 0.01),
        'w1': nrm(ks[19], (L, N_EXPERTS, D_MODEL, D_EXPERT), D_MODEL ** -0.5),
        'w3': nrm(ks[20], (L, N_EXPERTS, D_MODEL, D_EXPERT), D_MODEL ** -0.5),
        'w2': nrm(ks[21], (L, N_EXPERTS, D_EXPERT, D_MODEL), BETA * D_EXPERT ** -0.5),
        'ln2_g': 1.0 + nrm(ks[22], (L, D_MODEL), 0.02),
        'ln2_b': nrm(ks[23], (L, D_MODEL), 0.02),
    }


def reference(x, mem, mem_ln_g, mem_ln_b, w_in, attn_sinks, conv_dw, conv_dw_b,
              conv_ln_g, conv_ln_b, w_mem_kv, w_branch, w_out, ln1_g, ln1_b,
              router_group, router_group_b, router_expert, router_expert_b,
              w1, w3, w2, ln2_g, ln2_b):
    b, s = x.shape[0], x.shape[1]
    mem_n = layer_norm(mem, mem_ln_g, mem_ln_b)
    for l in range(DEPTH):
        proj = x @ w_in[l]
        q, k, v, u, qm, gates = jnp.split(proj, IN_SPLITS, axis=-1)
        q = q.reshape(b, s, N_Q_HEADS, HEAD_DIM)
        k = k.reshape(b, s, N_KV_HEADS, HEAD_DIM)
        v = v.reshape(b, s, N_KV_HEADS, HEAD_DIM)
        o_attn = sliding_window_attention(q, k, v, attn_sinks[l])
        o_conv = conformer_conv(u, conv_dw[l], conv_dw_b[l], conv_ln_g[l], conv_ln_b[l])
        o_mem = memory_attention(qm, mem_n, w_mem_kv[l])
        g = jax.nn.sigmoid(gates.astype(jnp.float32)).reshape(b, s, N_BRANCH, D_MODEL)
        merged = g[:, :, 0] * (o_attn @ w_branch[l, 0])
        merged = merged + g[:, :, 1] * (o_conv @ w_branch[l, 1])
        merged = merged + g[:, :, 2] * (o_mem @ w_branch[l, 2])
        y = merged.astype(x.dtype) @ w_out[l]
        x = layer_norm(ALPHA * x + y, ln1_g[l], ln1_b[l])
        f = hierarchical_moe(x, router_group[l], router_group_b[l], router_expert[l],
                             router_expert_b[l], w1[l], w3[l], w2[l])
        x = layer_norm(ALPHA * x + f, ln2_g[l], ln2_b[l])
    return x
```

```python
import functools

import jax
import jax.numpy as jnp
from jax import lax
from jax.experimental import pallas as pl
from jax.experimental.pallas import tpu as pltpu

F32 = jnp.float32
BF16 = jnp.bfloat16
U32 = jnp.uint32
I32 = jnp.int32

D_MODEL = 2048
BRANCH = 1024
HEAD_DIM = 64
N_Q_HEADS = 16
N_KV_HEADS = 4
GQA = 4
WINDOW = 128
BLOCK = 128
KV_WIDTH = N_KV_HEADS * HEAD_DIM
CONV_K = 31
CONV_HIST = 32
N_MEM = 256
MEM_HEADS = 4
MEM_HD = 256
N_GROUPS = 4
EPG = 8
N_EXPERTS = 32
D_EXPERT = 512
IN_WIDTH = 10752
ALPHA = (2.0 * 4) ** 0.25
LN_EPS = 1e-5
NEG_INF = -1e30

COL_Q, COL_A, COL_G, COL_QM, COL_GATES, COL_K, COL_V = 0, 1024, 2048, 3072, 4096, 10240, 10496

ROUTE_LANES = 128
EXPERT_TILE = 256
TOKEN_TILE = 256
VMEM_LIMIT = 56 * 1024 * 1024


def _cparams(sem, vmem=VMEM_LIMIT):
    return pltpu.CompilerParams(dimension_semantics=sem, vmem_limit_bytes=vmem)


def _layer_norm_rows(z, g, b):
    mu = jnp.mean(z, axis=-1, keepdims=True)
    c = z - mu
    var = jnp.mean(c * c, axis=-1, keepdims=True)
    return c * lax.rsqrt(var + LN_EPS) * g + b


def _pack_halves(x):
    n = x.shape[1] // 2
    lo = lax.bitcast_convert_type(x[:, :n].astype(BF16).astype(F32), U32)
    hi = lax.bitcast_convert_type(x[:, n:].astype(BF16).astype(F32), U32)
    return (lo >> 16) | hi


def _unpack_halves(u):
    lo = lax.bitcast_convert_type(u << 16, F32)
    hi = lax.bitcast_convert_type(u & jnp.uint32(0xFFFF0000), F32)
    return lo, hi


def _mm_kernel(x_ref, w_ref, o_ref):
    o_ref[...] = jnp.dot(x_ref[...], w_ref[...], preferred_element_type=F32).astype(o_ref.dtype)


def _matmul(x, w, out_dtype, tm, tn):
    m, k = x.shape
    n = w.shape[1]
    return pl.pallas_call(
        _mm_kernel,
        grid=(m // tm, n // tn),
        in_specs=[pl.BlockSpec((tm, k), lambda i, j: (i, 0)),
                  pl.BlockSpec((k, tn), lambda i, j: (0, j))],
        out_specs=pl.BlockSpec((tm, tn), lambda i, j: (i, j)),
        out_shape=jax.ShapeDtypeStruct((m, n), out_dtype),
        compiler_params=_cparams(("parallel", "parallel")),
    )(x, w)


def _ln_kernel(x_ref, g_ref, b_ref, o_ref):
    o_ref[...] = _layer_norm_rows(x_ref[...], g_ref[...], b_ref[...]).astype(o_ref.dtype)


def _ln_rows(x, g, b, out_dtype, tm):
    m, d = x.shape
    return pl.pallas_call(
        _ln_kernel,
        grid=(m // tm,),
        in_specs=[pl.BlockSpec((tm, d), lambda i: (i, 0)),
                  pl.BlockSpec((1, d), lambda i: (0, 0)),
                  pl.BlockSpec((1, d), lambda i: (0, 0))],
        out_specs=pl.BlockSpec((tm, d), lambda i: (i, 0)),
        out_shape=jax.ShapeDtypeStruct((m, d), out_dtype),
        compiler_params=_cparams(("parallel",)),
    )(x, g.reshape(1, d), b.reshape(1, d))


def _swa_kernel(sink_ref, q_ref, kc_ref, kp_ref, vc_ref, vp_ref, bm_ref, o_ref):
    first = pl.program_id(1) == 0
    k = jnp.concatenate([kp_ref[...], kc_ref[...]], axis=0)
    v = jnp.concatenate([vp_ref[...], vc_ref[...]], axis=0)
    col = lax.broadcasted_iota(I32, (BLOCK, 2 * BLOCK), 1)
    no_prev = first & (col < BLOCK)
    for h in range(N_Q_HEADS):
        kv = h // GQA
        qh = q_ref[:, h * HEAD_DIM:(h + 1) * HEAD_DIM]
        kh = k[:, kv * HEAD_DIM:(kv + 1) * HEAD_DIM]
        vh = v[:, kv * HEAD_DIM:(kv + 1) * HEAD_DIM]
        s = lax.dot_general(qh, kh, (((1,), (1,)), ((), ())), preferred_element_type=F32)
        s = s * (HEAD_DIM ** -0.5) + bm_ref[h]
        s = jnp.where(no_prev, NEG_INF, s)
        sink = sink_ref[h]
        m = jnp.maximum(jnp.max(s, axis=-1, keepdims=True), sink)
        p = jnp.exp(s - m)
        denom = jnp.sum(p, axis=-1, keepdims=True) + jnp.exp(sink - m)
        o = jnp.dot(p.astype(BF16), vh, preferred_element_type=F32) / denom
        o_ref[:, h * HEAD_DIM:(h + 1) * HEAD_DIM] = o.astype(o_ref.dtype)


def _swa_bias_mask():
    qi = jnp.arange(BLOCK)[:, None] + BLOCK
    si = jnp.arange(2 * BLOCK)[None, :]
    dist = qi - si
    valid = (dist >= 0) & (dist < WINDOW)
    hh = jnp.arange(1, N_Q_HEADS + 1, dtype=F32)
    slopes = jnp.exp2(-8.0 * hh / N_Q_HEADS)
    bias = -slopes[:, None, None] * dist.astype(F32)[None]
    return jnp.where(valid[None], bias, NEG_INF)


def _swa(proj, sinks, bias_mask, batch, seq):
    nb = seq // BLOCK
    kcol, vcol = COL_K // KV_WIDTH, COL_V // KV_WIDTH

    def cur(b, n):
        return b * nb + n

    def prev(b, n):
        return b * nb + jnp.maximum(n - 1, 0)

    return pl.pallas_call(
        _swa_kernel,
        grid=(batch, nb),
        in_specs=[pl.BlockSpec(memory_space=pltpu.SMEM),
                  pl.BlockSpec((BLOCK, BRANCH), lambda b, n: (cur(b, n), COL_Q // BRANCH)),
                  pl.BlockSpec((BLOCK, KV_WIDTH), lambda b, n: (cur(b, n), kcol)),
                  pl.BlockSpec((BLOCK, KV_WIDTH), lambda b, n: (prev(b, n), kcol)),
                  pl.BlockSpec((BLOCK, KV_WIDTH), lambda b, n: (cur(b, n), vcol)),
                  pl.BlockSpec((BLOCK, KV_WIDTH), lambda b, n: (prev(b, n), vcol)),
                  pl.BlockSpec((N_Q_HEADS, BLOCK, 2 * BLOCK), lambda b, n: (0, 0, 0))],
        out_specs=pl.BlockSpec((BLOCK, BRANCH), lambda b, n: (cur(b, n), 0)),
        out_shape=jax.ShapeDtypeStruct((batch * seq, BRANCH), BF16),
        compiler_params=_cparams(("parallel", "arbitrary")),
    )(sinks, proj, proj, proj, proj, proj, bias_mask)


CONV_TILE = 256
CONV_CHUNK = 32


def _conv_kernel(ac_ref, gc_ref, ah_ref, gh_ref, w_ref, b_ref, lg_ref, lb_ref, o_ref, h_ref, c_ref):
    first = pl.program_id(1) == 0
    hist = ah_ref[...].astype(F32) * jax.nn.sigmoid(gh_ref[...].astype(F32))
    h_ref[0:CONV_HIST, :] = jnp.where(first, 0.0, hist)
    h_ref[CONV_HIST:, :] = ac_ref[...].astype(F32) * jax.nn.sigmoid(gc_ref[...].astype(F32))
    base = CONV_HIST - (CONV_K - 1)
    for c in range(CONV_TILE // CONV_CHUNK):
        r0 = c * CONV_CHUNK
        acc = jnp.broadcast_to(b_ref[...], (CONV_CHUNK, BRANCH))
        for j in range(CONV_K):
            acc = acc + w_ref[j:j + 1, :] * h_ref[base + r0 + j: base + r0 + j + CONV_CHUNK, :]
        c_ref[r0:r0 + CONV_CHUNK, :] = acc
    y = _layer_norm_rows(c_ref[...], lg_ref[...], lb_ref[...])
    o_ref[...] = (y * jax.nn.sigmoid(y)).astype(o_ref.dtype)


def _conv_branch(proj, w_dw, b_dw, ln_g, ln_b, batch, seq):
    nt = seq // CONV_TILE
    hist_per_tile = CONV_TILE // CONV_HIST
    w_pad = jnp.zeros((CONV_HIST, BRANCH), F32).at[:CONV_K].set(w_dw)

    def cur(b, n):
        return b * nt + n

    def hist(b, n):
        return jnp.maximum((b * nt + n) * hist_per_tile - 1, 0)

    vec = lambda: pl.BlockSpec((1, BRANCH), lambda b, n: (0, 0))
    return pl.pallas_call(
        _conv_kernel,
        grid=(batch, nt),
        in_specs=[pl.BlockSpec((CONV_TILE, BRANCH), lambda b, n: (cur(b, n), COL_A // BRANCH)),
                  pl.BlockSpec((CONV_TILE, BRANCH), lambda b, n: (cur(b, n), COL_G // BRANCH)),
                  pl.BlockSpec((CONV_HIST, BRANCH), lambda b, n: (hist(b, n), COL_A // BRANCH)),
                  pl.BlockSpec((CONV_HIST, BRANCH), lambda b, n: (hist(b, n), COL_G // BRANCH)),
                  pl.BlockSpec((CONV_HIST, BRANCH), lambda b, n: (0, 0)),
                  vec(), vec(), vec()],
        out_specs=pl.BlockSpec((CONV_TILE, BRANCH), lambda b, n: (cur(b, n), 0)),
        out_shape=jax.ShapeDtypeStruct((batch * seq, BRANCH), BF16),
        scratch_shapes=[pltpu.VMEM((CONV_HIST + CONV_TILE, BRANCH), F32),
                        pltpu.VMEM((CONV_TILE, BRANCH), F32)],
        compiler_params=_cparams(("parallel", "arbitrary")),
    )(proj, proj, proj, proj, w_pad, b_dw.reshape(1, BRANCH), ln_g.reshape(1, BRANCH),
      ln_b.reshape(1, BRANCH))


MEM_TILE = 512


def _mem_attn_kernel(q_ref, k_ref, v_ref, o_ref):
    for h in range(MEM_HEADS):
        sl = slice(h * MEM_HD, (h + 1) * MEM_HD)
        s = lax.dot_general(q_ref[:, sl], k_ref[:, sl], (((1,), (1,)), ((), ())),
                            preferred_element_type=F32) * (MEM_HD ** -0.5)
        m = jnp.max(s, axis=-1, keepdims=True)
        p = jnp.exp(s - m)
        denom = jnp.sum(p, axis=-1, keepdims=True)
        o = jnp.dot(p.astype(BF16), v_ref[:, sl], preferred_element_type=F32) / denom
        o_ref[:, sl] = o.astype(o_ref.dtype)


def _mem_attn(proj, kv, batch, seq):
    nt = seq // MEM_TILE
    return pl.pallas_call(
        _mem_attn_kernel,
        grid=(batch, nt),
        in_specs=[pl.BlockSpec((MEM_TILE, BRANCH), lambda b, n: (b * nt + n, COL_QM // BRANCH)),
                  pl.BlockSpec((N_MEM, BRANCH), lambda b, n: (b, 0)),
                  pl.BlockSpec((N_MEM, BRANCH), lambda b, n: (b, 1))],
        out_specs=pl.BlockSpec((MEM_TILE, BRANCH), lambda b, n: (b * nt + n, 0)),
        out_shape=jax.ShapeDtypeStruct((batch * seq, BRANCH), BF16),
        compiler_params=_cparams(("parallel", "parallel")),
    )(proj, kv, kv)


def _route(logits):
    lane = lax.broadcasted_iota(I32, logits.shape, 1)
    lane_f = lane.astype(F32)
    big = jnp.float32(1e9)
    low = jnp.float32(-3e38)
    gmask = lane < N_GROUPS
    gmax = jnp.max(jnp.where(gmask, logits, low), axis=-1, keepdims=True)
    grp = jnp.min(jnp.where(gmask & (logits == gmax), lane_f, big), axis=-1, keepdims=True)
    p_grp = 1.0 / jnp.sum(jnp.where(gmask, jnp.exp(logits - gmax), 0.0), axis=-1, keepdims=True)
    lo = N_GROUPS + grp * EPG
    emask = (lane_f >= lo) & (lane_f < lo + EPG)
    el = jnp.where(emask, logits, low)
    e1 = jnp.max(el, axis=-1, keepdims=True)
    i1 = jnp.min(jnp.where(emask & (logits == e1), lane_f, big), axis=-1, keepdims=True)
    rest = emask & (lane_f != i1)
    e2 = jnp.max(jnp.where(rest, logits, low), axis=-1, keepdims=True)
    i2 = jnp.min(jnp.where(rest & (logits == e2), lane_f, big), axis=-1, keepdims=True)
    t = jnp.exp(e2 - e1)
    w0 = p_grp / (1.0 + t)
    w1 = p_grp * t / (1.0 + t)
    ids = jnp.where(lane == 0, i1 - N_GROUPS, jnp.where(lane == 1, i2 - N_GROUPS, 0.0)).astype(I32)
    wts = jnp.where(lane == 0, w0, jnp.where(lane == 1, w1, 0.0))
    return ids, wts


def _merge_kernel(oa_ref, oc_ref, om_ref, g0_ref, g1_ref, g2_ref, x_ref, wb_ref, wo_ref,
                  lg_ref, lb_ref, wrh_ref, wrl_ref, br_ref, x1_ref, x1p_ref, ids_ref, wts_ref):
    merged = jax.nn.sigmoid(g0_ref[...].astype(F32)) * jnp.dot(
        oa_ref[...], wb_ref[0], preferred_element_type=F32)
    merged = merged + jax.nn.sigmoid(g1_ref[...].astype(F32)) * jnp.dot(
        oc_ref[...], wb_ref[1], preferred_element_type=F32)
    merged = merged + jax.nn.sigmoid(g2_ref[...].astype(F32)) * jnp.dot(
        om_ref[...], wb_ref[2], preferred_element_type=F32)
    y = jnp.dot(merged.astype(BF16), wo_ref[...], preferred_element_type=F32)
    x1 = _layer_norm_rows(ALPHA * x_ref[...] + y, lg_ref[...], lb_ref[...])
    x1_ref[...] = x1
    x1p_ref[...] = _pack_halves(x1)
    xh = x1.astype(BF16)
    xl = (x1 - xh.astype(F32)).astype(BF16)
    logits = (jnp.dot(xh, wrh_ref[...], preferred_element_type=F32)
              + jnp.dot(xl, wrh_ref[...], preferred_element_type=F32)
              + jnp.dot(xh, wrl_ref[...], preferred_element_type=F32)) + br_ref[...]
    ids, wts = _route(logits)
    ids_ref[...] = ids
    wts_ref[...] = wts


def _merge(o_attn, o_conv, o_mem, proj, x, w_branch, w_out, ln_g, ln_b, wr_hi, wr_lo, b_r):
    t = x.shape[0]
    tm = TOKEN_TILE
    gcol = COL_GATES // D_MODEL
    row = lambda w: pl.BlockSpec((tm, w), lambda i: (i, 0))
    const = lambda shape: pl.BlockSpec(shape, lambda i: (0,) * len(shape), pipeline_mode=pl.Buffered(1))
    return pl.pallas_call(
        _merge_kernel,
        grid=(t // tm,),
        in_specs=[row(BRANCH), row(BRANCH), row(BRANCH),
                  pl.BlockSpec((tm, D_MODEL), lambda i: (i, gcol)),
                  pl.BlockSpec((tm, D_MODEL), lambda i: (i, gcol + 1)),
                  pl.BlockSpec((tm, D_MODEL), lambda i: (i, gcol + 2)),
                  row(D_MODEL),
                  const((3, BRANCH, D_MODEL)), const((D_MODEL, D_MODEL)),
                  const((1, D_MODEL)), const((1, D_MODEL)),
                  const((D_MODEL, ROUTE_LANES)), const((D_MODEL, ROUTE_LANES)), const((1, ROUTE_LANES))],
        out_specs=[row(D_MODEL), row(D_MODEL // 2), row(ROUTE_LANES), row(ROUTE_LANES)],
        out_shape=[jax.ShapeDtypeStruct((t, D_MODEL), F32),
                   jax.ShapeDtypeStruct((t, D_MODEL // 2), U32),
                   jax.ShapeDtypeStruct((t, ROUTE_LANES), I32),
                   jax.ShapeDtypeStruct((t, ROUTE_LANES), F32)],
        compiler_params=_cparams(("parallel",)),
    )(o_attn, o_conv, o_mem, proj, proj, proj, x, w_branch, w_out,
      ln_g.reshape(1, D_MODEL), ln_b.reshape(1, D_MODEL), wr_hi, wr_lo, b_r)


def _row_copy(src, src_row, dst, dst_row, sem):
    return pltpu.make_async_copy(src.at[pl.ds(src_row, 1), :], dst.at[pl.ds(dst_row, 1), :], sem)


def _dispatch_kernel(pos_ref, x_ref, xs_in_ref, xs_ref, sem):
    del xs_in_ref
    tm = x_ref.shape[0]

    def start(r, carry):
        for k in range(2):
            _row_copy(x_ref, r, xs_ref, pos_ref[0, 0, 2 * r + k], sem).start()
        return carry

    lax.fori_loop(0, tm, start, 0, unroll=8)
    for _ in range(2 * tm):
        _row_copy(x_ref, 0, xs_ref, 0, sem).wait()


def _dispatch(x1p, pos, n_slots):
    t, w = x1p.shape
    tm = TOKEN_TILE
    xs0 = jnp.zeros((n_slots, w), U32)
    return pl.pallas_call(
        _dispatch_kernel,
        grid=(t // tm,),
        in_specs=[pl.BlockSpec((1, 1, 2 * tm), lambda i: (i, 0, 0), memory_space=pltpu.SMEM),
                  pl.BlockSpec((tm, w), lambda i: (i, 0)),
                  pl.BlockSpec(memory_space=pl.ANY)],
        out_specs=pl.BlockSpec(memory_space=pl.ANY),
        out_shape=jax.ShapeDtypeStruct((n_slots, w), U32),
        scratch_shapes=[pltpu.SemaphoreType.DMA(())],
        input_output_aliases={2: 0},
        compiler_params=pltpu.CompilerParams(dimension_semantics=("arbitrary",),
                                             vmem_limit_bytes=VMEM_LIMIT, has_side_effects=True),
    )(pos.reshape(t // tm, 1, 2 * tm), x1p, xs0)


def _expert_kernel(te_ref, nu_ref, xs_ref, w1_ref, w3_ref, w2_ref, ys_ref):
    del te_ref
    used = pl.program_id(0) < nu_ref[0]

    @pl.when(used)
    def _():
        half = D_MODEL // 2
        lo, hi = _unpack_halves(xs_ref[...])
        lo = lo.astype(BF16)
        hi = hi.astype(BF16)
        a = (jnp.dot(lo, w1_ref[0, :half, :], preferred_element_type=F32)
             + jnp.dot(hi, w1_ref[0, half:, :], preferred_element_type=F32))
        b = (jnp.dot(lo, w3_ref[0, :half, :], preferred_element_type=F32)
             + jnp.dot(hi, w3_ref[0, half:, :], preferred_element_type=F32))
        h = (a * jax.nn.sigmoid(a)) * b
        y = jnp.dot(h.astype(BF16), w2_ref[0], preferred_element_type=F32)
        ys_ref[...] = _pack_halves(y)

    @pl.when(jnp.logical_not(used))
    def _():
        ys_ref[...] = jnp.zeros_like(ys_ref)


def _experts(xs, tile_expert, n_used, w1, w3, w2):
    n_slots, w = xs.shape
    te = EXPERT_TILE
    grid_spec = pltpu.PrefetchScalarGridSpec(
        num_scalar_prefetch=2,
        grid=(n_slots // te,),
        in_specs=[pl.BlockSpec((te, w), lambda i, te_r, nu_r: (jnp.minimum(i, nu_r[0] - 1), 0)),
                  pl.BlockSpec((1, D_MODEL, D_EXPERT), lambda i, te_r, nu_r: (te_r[i], 0, 0)),
                  pl.BlockSpec((1, D_MODEL, D_EXPERT), lambda i, te_r, nu_r: (te_r[i], 0, 0)),
                  pl.BlockSpec((1, D_EXPERT, D_MODEL), lambda i, te_r, nu_r: (te_r[i], 0, 0))],
        out_specs=pl.BlockSpec((te, w), lambda i, te_r, nu_r: (i, 0)),
    )
    return pl.pallas_call(
        _expert_kernel,
        grid_spec=grid_spec,
        out_shape=jax.ShapeDtypeStruct((n_slots, w), U32),
        compiler_params=_cparams(("arbitrary",)),
    )(tile_expert, n_used, xs, w1, w3, w2)


def _combine_kernel(pos_ref, posn_ref, wts_ref, x1_ref, lg_ref, lb_ref, ys_ref,
                    x2_ref, x2b_ref, buf_ref, sem):
    i = pl.program_id(0)
    n = pl.num_programs(0)
    tm = x1_ref.shape[0]
    slot = i % 2

    def issue(p_ref, s):
        def start(r, carry):
            for k in range(2):
                _row_copy(ys_ref, p_ref[0, 0, 2 * r + k], buf_ref.at[s, k], r, sem.at[s]).start()
            return carry
        lax.fori_loop(0, tm, start, 0, unroll=8)

    @pl.when(i == 0)
    def _():
        issue(pos_ref, 0)

    @pl.when(i + 1 < n)
    def _():
        issue(posn_ref, 1 - slot)

    for _ in range(2 * tm):
        _row_copy(ys_ref, 0, buf_ref.at[slot, 0], 0, sem.at[slot]).wait()

    lo0, hi0 = _unpack_halves(buf_ref[slot, 0])
    lo1, hi1 = _unpack_halves(buf_ref[slot, 1])
    w0 = wts_ref[:, 0:1]
    w1 = wts_ref[:, 1:2]
    f = jnp.concatenate([w0 * lo0 + w1 * lo1, w0 * hi0 + w1 * hi1], axis=1)
    x2 = _layer_norm_rows(ALPHA * x1_ref[...] + f, lg_ref[...], lb_ref[...])
    x2_ref[...] = x2
    x2b_ref[...] = x2.astype(BF16)


def _combine(ys, pos, wts, x1, ln_g, ln_b):
    t = x1.shape[0]
    tm = TOKEN_TILE
    nt = t // tm
    w = ys.shape[1]
    pos3 = pos.reshape(nt, 1, 2 * tm)
    row = lambda width: pl.BlockSpec((tm, width), lambda i: (i, 0))
    vec = lambda: pl.BlockSpec((1, D_MODEL), lambda i: (0, 0))
    return pl.pallas_call(
        _combine_kernel,
        grid=(nt,),
        in_specs=[pl.BlockSpec((1, 1, 2 * tm), lambda i: (i, 0, 0), memory_space=pltpu.SMEM),
                  pl.BlockSpec((1, 1, 2 * tm), lambda i: (jnp.minimum(i + 1, nt - 1), 0, 0),
                               memory_space=pltpu.SMEM),
                  row(ROUTE_LANES), row(D_MODEL), vec(), vec(),
                  pl.BlockSpec(memory_space=pl.ANY)],
        out_specs=[row(D_MODEL), row(D_MODEL)],
        out_shape=[jax.ShapeDtypeStruct((t, D_MODEL), F32),
                   jax.ShapeDtypeStruct((t, D_MODEL), BF16)],
        scratch_shapes=[pltpu.VMEM((2, 2, tm, w), U32), pltpu.SemaphoreType.DMA((2,))],
        compiler_params=_cparams(("arbitrary",)),
    )(pos3, pos3, wts, x1, ln_g.reshape(1, D_MODEL), ln_b.reshape(1, D_MODEL), ys)


def _dispatch_plan(ids, n_tiles):
    e = ids[:, :2].reshape(-1)
    onehot = (e[:, None] == jnp.arange(N_EXPERTS, dtype=I32)[None, :]).astype(I32)
    csum = jnp.cumsum(onehot, axis=0)
    rank = jnp.take_along_axis(csum, e[:, None], axis=1)[:, 0] - 1
    counts = csum[-1]
    padded = ((counts + EXPERT_TILE - 1) // EXPERT_TILE) * EXPERT_TILE
    ends = jnp.cumsum(padded)
    starts = ends - padded
    pos = (starts[e] + rank).astype(I32)
    n_used = (ends[-1] // EXPERT_TILE).astype(I32)
    tile_start = jnp.arange(n_tiles, dtype=I32) * EXPERT_TILE
    tile_e = jnp.minimum(jnp.searchsorted(ends, tile_start, side="right"), N_EXPERTS - 1).astype(I32)
    last_e = tile_e[jnp.maximum(n_used - 1, 0)]
    tile_e = jnp.where(jnp.arange(n_tiles) < n_used, tile_e, last_e)
    return pos, tile_e, n_used.reshape(1)


def _permute_in_proj(w):
    return jnp.concatenate([w[:, 0:1024], w[:, 1536:3584], w[:, 3584:4608], w[:, 4608:10752],
                            w[:, 1024:1536]], axis=1)


def kernel(x, mem, mem_ln_g, mem_ln_b, w_in, attn_sinks, conv_dw, conv_dw_b, conv_ln_g, conv_ln_b,
           w_mem_kv, w_branch, w_out, ln1_g, ln1_b, router_group, router_group_b, router_expert,
           router_expert_b, w1, w3, w2, ln2_g, ln2_b):
    batch, seq, d = x.shape
    depth = w_in.shape[0]
    t = batch * seq
    n_slots = 2 * t + N_EXPERTS * EXPERT_TILE
    n_tiles = n_slots // EXPERT_TILE

    xf = x.reshape(t, d)
    xb = xf.astype(BF16)
    mem_n = _ln_rows(mem.reshape(batch * N_MEM, d), mem_ln_g, mem_ln_b, BF16, 512)
    bias_mask = _swa_bias_mask()

    for l in range(depth):
        w_in_l = _permute_in_proj(w_in[l]).astype(BF16)
        proj = _matmul(xb, w_in_l, BF16, 1024, 512)
        o_attn = _swa(proj, attn_sinks[l], bias_mask, batch, seq)
        o_conv = _conv_branch(proj, conv_dw[l], conv_dw_b[l], conv_ln_g[l], conv_ln_b[l], batch, seq)
        kv = _matmul(mem_n, w_mem_kv[l].astype(BF16), BF16, 512, 512)
        o_mem = _mem_attn(proj, kv, batch, seq)

        w_r = jnp.zeros((d, ROUTE_LANES), F32)
        w_r = w_r.at[:, :N_GROUPS].set(router_group[l]).at[:, N_GROUPS:N_GROUPS + N_EXPERTS].set(
            router_expert[l])
        b_r = jnp.zeros((1, ROUTE_LANES), F32)
        b_r = b_r.at[0, :N_GROUPS].set(router_group_b[l]).at[0, N_GROUPS:N_GROUPS + N_EXPERTS].set(
            router_expert_b[l])
        wr_hi = w_r.astype(BF16)
        wr_lo = (w_r - wr_hi.astype(F32)).astype(BF16)
        x1, x1p, ids, wts = _merge(o_attn, o_conv, o_mem, proj, xf, w_branch[l].astype(BF16),
                                   w_out[l].astype(BF16), ln1_g[l], ln1_b[l], wr_hi, wr_lo, b_r)

        pos, tile_e, n_used = _dispatch_plan(ids, n_tiles)
        xs = _dispatch(x1p, pos, n_slots)
        ys = _experts(xs, tile_e, n_used, w1[l].astype(BF16), w3[l].astype(BF16), w2[l].astype(BF16))
        xf, xb = _combine(ys, pos, wts, x1, ln2_g[l], ln2_b[l])

    return xf.reshape(batch, seq, d)
```

```python
import jax
import jax.numpy as jnp
from jax import lax
from jax.experimental import pallas as pl
from jax.experimental.pallas import tpu as pltpu

F32 = jnp.float32
BF16 = jnp.bfloat16
U32 = jnp.uint32
I32 = jnp.int32

D_MODEL = 2048
BRANCH = 1024
HEAD_DIM = 64
N_Q_HEADS = 16
N_KV_HEADS = 4
GQA = 4
WINDOW = 128
BLOCK = 128
KV_WIDTH = N_KV_HEADS * HEAD_DIM
CONV_K = 31
CONV_HIST = 32
N_MEM = 256
MEM_HEADS = 4
MEM_HD = 256
N_GROUPS = 4
EPG = 8
N_EXPERTS = 32
D_EXPERT = 512
ALPHA = (2.0 * 4) ** 0.25
LN_EPS = 1e-5
NEG_INF = -1e30
LOG2E = 1.4426950408889634

LANES = 128
SUBLANES = 8

COL_Q, COL_A, COL_G, COL_QM, COL_GATES, COL_K, COL_V = 0, 1024, 2048, 3072, 4096, 10240, 10496
IN_TILE_N = 512
IN_TILE_M = 2048

EXPERT_TILE = 256
TOKEN_TILE = 256
VMEM_LIMIT = 56 * 1024 * 1024


def _cparams(sem, vmem=VMEM_LIMIT):
    return pltpu.CompilerParams(dimension_semantics=sem, vmem_limit_bytes=vmem)


def _layer_norm_rows(z, g, b):
    mu = jnp.mean(z, axis=-1, keepdims=True)
    c = z - mu
    var = jnp.mean(c * c, axis=-1, keepdims=True)
    return c * lax.rsqrt(var + LN_EPS) * g + b


def _pack_halves(x):
    n = x.shape[1] // 2
    lo = lax.bitcast_convert_type(x[:, :n].astype(BF16).astype(F32), U32)
    hi = lax.bitcast_convert_type(x[:, n:].astype(BF16).astype(F32), U32)
    return (lo >> 16) | hi


def _unpack_halves(u):
    lo = lax.bitcast_convert_type(u << 16, F32)
    hi = lax.bitcast_convert_type(u & jnp.uint32(0xFFFF0000), F32)
    return lo, hi


def _proj_kernel(x_ref, w_ref, o_ref):
    o_ref[...] = jnp.dot(x_ref[...], w_ref[...].astype(BF16),
                         preferred_element_type=F32).astype(o_ref.dtype)


def _project(x, w_all, layer, col_block, tm, tn, name):
    m, k = x.shape
    n = w_all.shape[2]
    return pl.pallas_call(
        _proj_kernel,
        grid=(m // tm, n // tn),
        in_specs=[pl.BlockSpec((tm, k), lambda i, j: (i, 0)),
                  pl.BlockSpec((None, k, tn), lambda i, j: (layer, 0, col_block(j)))],
        out_specs=pl.BlockSpec((tm, tn), lambda i, j: (i, j)),
        out_shape=jax.ShapeDtypeStruct((m, n), BF16),
        compiler_params=_cparams(("parallel", "parallel")),
        name=name,
    )(x, w_all)


def _in_proj_col_block(j):
    kv_src = 1024 // IN_TILE_N
    n_tiles = 10752 // IN_TILE_N
    return jnp.where(j < kv_src, j, jnp.where(j < n_tiles - 1, j + 1, kv_src))


def _ln_kernel(x_ref, g_ref, b_ref, o_ref):
    o_ref[...] = _layer_norm_rows(x_ref[...], g_ref[...], b_ref[...]).astype(o_ref.dtype)


def _ln_rows(x, g, b, out_dtype, tm):
    m, d = x.shape
    return pl.pallas_call(
        _ln_kernel,
        grid=(m // tm,),
        in_specs=[pl.BlockSpec((tm, d), lambda i: (i, 0)),
                  pl.BlockSpec((1, d), lambda i: (0, 0)),
                  pl.BlockSpec((1, d), lambda i: (0, 0))],
        out_specs=pl.BlockSpec((tm, d), lambda i: (i, 0)),
        out_shape=jax.ShapeDtypeStruct((m, d), out_dtype),
        compiler_params=_cparams(("parallel",)),
        name="mem_ln",
    )(x, g.reshape(1, d), b.reshape(1, d))


def _swap_halves(t):
    u = pltpu.bitcast(t, U32)
    return pltpu.bitcast(pltpu.roll(u, LANES // 2, axis=1), BF16)


def _swa_kernel(q_ref, kc_ref, kp_ref, vc_ref, vp_ref, bm_ref, o_ref):
    low = lax.broadcasted_iota(I32, (1, LANES), 1) < HEAD_DIM
    sink_row = lax.broadcasted_iota(I32, (2 * BLOCK, 1), 0) == 0
    zero = jnp.zeros((), BF16)
    scale = (HEAD_DIM ** -0.5) * LOG2E
    for kv in range(N_KV_HEADS):
        tile = slice((kv // 2) * LANES, (kv // 2 + 1) * LANES)
        kt = jnp.concatenate([kp_ref[:, tile], kc_ref[:, tile]], axis=0)
        vt = jnp.concatenate([vp_ref[:, tile], vc_ref[:, tile]], axis=0)
        kt = jnp.where(sink_row, zero, kt)
        vt = jnp.where(sink_row, zero, vt)
        ks, vs = _swap_halves(kt), _swap_halves(vt)
        k_lo, k_hi = (kt, ks) if kv % 2 == 0 else (ks, kt)
        v_lo, v_hi = (vt, vs) if kv % 2 == 0 else (vs, vt)
        kk = jnp.where(low, k_lo, k_hi)
        vv = (jnp.where(low, v_lo, zero), jnp.where(low, zero, v_hi))
        qts = [q_ref[:, (2 * kv + t) * LANES:(2 * kv + t + 1) * LANES] for t in range(2)]
        q4 = jnp.concatenate([jnp.where(low, qts[0], zero), jnp.where(low, zero, qts[0]),
                              jnp.where(low, qts[1], zero), jnp.where(low, zero, qts[1])], axis=0)
        s4 = lax.dot_general(q4, kk, (((1,), (1,)), ((), ())), preferred_element_type=F32)
        s4 = s4 * scale + bm_ref[0, kv]
        for t in range(2):
            acc = None
            for half in range(2):
                g = 2 * t + half
                s = s4[g * BLOCK:(g + 1) * BLOCK]
                p = jnp.exp2(s - jnp.max(s, axis=-1, keepdims=True))
                denom = jnp.sum(p, axis=-1, keepdims=True)
                o = jnp.dot(p.astype(BF16), vv[half], preferred_element_type=F32) / denom
                acc = o if acc is None else acc + o
            o_ref[:, (2 * kv + t) * LANES:(2 * kv + t + 1) * LANES] = acc.astype(o_ref.dtype)


def _swa_bias_mask(sinks):
    qi = jnp.arange(BLOCK)[:, None] + BLOCK
    si = jnp.arange(2 * BLOCK)[None, :]
    dist = qi - si
    valid = (dist >= 0) & (dist < WINDOW)
    hh = jnp.arange(1, N_Q_HEADS + 1, dtype=F32)
    slopes = jnp.exp2(-8.0 * hh / N_Q_HEADS)
    bias = -slopes[:, None, None] * dist.astype(F32)[None]
    later = jnp.where(valid[None], bias, NEG_INF)
    first = jnp.where((valid & (si >= BLOCK))[None], bias, NEG_INF)
    both = jnp.stack([first, later])
    both = jnp.where((si == 0)[None, None], sinks.astype(F32)[None, :, None, None], both) * LOG2E
    return both.reshape(2, N_KV_HEADS, GQA * BLOCK, 2 * BLOCK)


def _swa(proj, bias_mask, batch, seq):
    nb = seq // BLOCK
    kcol, vcol = COL_K // KV_WIDTH, COL_V // KV_WIDTH

    def cur(b, n):
        return b * nb + n

    def prev(b, n):
        return b * nb + jnp.maximum(n - 1, 0)

    return pl.pallas_call(
        _swa_kernel,
        grid=(batch, nb),
        in_specs=[pl.BlockSpec((BLOCK, BRANCH), lambda b, n: (cur(b, n), COL_Q // BRANCH)),
                  pl.BlockSpec((BLOCK, KV_WIDTH), lambda b, n: (cur(b, n), kcol)),
                  pl.BlockSpec((BLOCK, KV_WIDTH), lambda b, n: (prev(b, n), kcol)),
                  pl.BlockSpec((BLOCK, KV_WIDTH), lambda b, n: (cur(b, n), vcol)),
                  pl.BlockSpec((BLOCK, KV_WIDTH), lambda b, n: (prev(b, n), vcol)),
                  pl.BlockSpec((1, N_KV_HEADS, GQA * BLOCK, 2 * BLOCK),
                               lambda b, n: (jnp.minimum(n, 1), 0, 0, 0))],
        out_specs=pl.BlockSpec((BLOCK, BRANCH), lambda b, n: (cur(b, n), 0)),
        out_shape=jax.ShapeDtypeStruct((batch * seq, BRANCH), BF16),
        compiler_params=_cparams(("parallel", "arbitrary")),
        name="swa",
    )(proj, proj, proj, proj, proj, bias_mask)


CONV_TILE = 256
CONV_CHUNK = 64
CONV_ROWS = CONV_HIST + CONV_TILE


def _conv_kernel(ac_ref, gc_ref, ah_ref, gh_ref, w_ref, b_ref, lg_ref, lb_ref, o_ref, h_ref, c_ref):
    first = pl.program_id(1) == 0
    hist = ah_ref[...].astype(F32) * jax.nn.sigmoid(gh_ref[...].astype(F32))
    h_ref[0, 0:CONV_HIST, :] = jnp.where(first, 0.0, hist)
    h_ref[0, CONV_HIST:, :] = ac_ref[...].astype(F32) * jax.nn.sigmoid(gc_ref[...].astype(F32))
    span = CONV_ROWS - SUBLANES
    for s in range(1, SUBLANES):
        h_ref[s, 0:span, :] = h_ref[0, s:s + span, :]
    base = CONV_HIST - (CONV_K - 1)
    groups = CONV_CHUNK // SUBLANES
    for lt in range(BRANCH // LANES):
        cols = slice(lt * LANES, (lt + 1) * LANES)
        taps = [w_ref[j * SUBLANES:(j + 1) * SUBLANES, cols] for j in range(CONV_K)]
        bias = jnp.broadcast_to(b_ref[:, cols], (groups, SUBLANES, LANES))

        def chunk(i, carry, cols=cols, taps=taps, bias=bias):
            r0 = pl.multiple_of(i * CONV_CHUNK, CONV_CHUNK)
            acc = bias
            for s in range(SUBLANES):
                offs = [off for off in range(base, base + CONV_K) if off % SUBLANES == s]
                q0, q1 = offs[0] // SUBLANES, offs[-1] // SUBLANES
                rows = (q1 - q0) * SUBLANES + CONV_CHUNK
                win = h_ref[s, pl.ds(r0 + q0 * SUBLANES, rows), cols]
                win = win.reshape(rows // SUBLANES, SUBLANES, LANES)
                for off in offs:
                    q = off // SUBLANES - q0
                    acc = acc + taps[off - base][None] * win[q:q + groups]
            c_ref[pl.ds(r0, CONV_CHUNK), cols] = acc.reshape(CONV_CHUNK, LANES)
            return carry

        lax.fori_loop(0, CONV_TILE // CONV_CHUNK, chunk, 0)
    y = _layer_norm_rows(c_ref[...], lg_ref[...], lb_ref[...])
    o_ref[...] = (y * jax.nn.sigmoid(y)).astype(o_ref.dtype)


def _conv_branch(proj, w_dw, b_dw, ln_g, ln_b, batch, seq):
    nt = seq // CONV_TILE
    hist_per_tile = CONV_TILE // CONV_HIST
    w_rep = jnp.repeat(w_dw, SUBLANES, axis=0)

    def cur(b, n):
        return b * nt + n

    def hist(b, n):
        return jnp.maximum((b * nt + n) * hist_per_tile - 1, 0)

    vec = lambda: pl.BlockSpec((1, BRANCH), lambda b, n: (0, 0))
    return pl.pallas_call(
        _conv_kernel,
        grid=(batch, nt),
        in_specs=[pl.BlockSpec((CONV_TILE, BRANCH), lambda b, n: (cur(b, n), COL_A // BRANCH)),
                  pl.BlockSpec((CONV_TILE, BRANCH), lambda b, n: (cur(b, n), COL_G // BRANCH)),
                  pl.BlockSpec((CONV_HIST, BRANCH), lambda b, n: (hist(b, n), COL_A // BRANCH)),
                  pl.BlockSpec((CONV_HIST, BRANCH), lambda b, n: (hist(b, n), COL_G // BRANCH)),
                  pl.BlockSpec((CONV_K * SUBLANES, BRANCH), lambda b, n: (0, 0)),
                  vec(), vec(), vec()],
        out_specs=pl.BlockSpec((CONV_TILE, BRANCH), lambda b, n: (cur(b, n), 0)),
        out_shape=jax.ShapeDtypeStruct((batch * seq, BRANCH), BF16),
        scratch_shapes=[pltpu.VMEM((SUBLANES, CONV_ROWS, BRANCH), F32),
                        pltpu.VMEM((CONV_TILE, BRANCH), F32)],
        compiler_params=_cparams(("parallel", "arbitrary")),
        name="conv",
    )(proj, proj, proj, proj, w_rep, b_dw.reshape(1, BRANCH), ln_g.reshape(1, BRANCH),
      ln_b.reshape(1, BRANCH))


MEM_TILE = 512


def _mem_attn_kernel(q_ref, k_ref, v_ref, o_ref):
    for h in range(MEM_HEADS):
        sl = slice(h * MEM_HD, (h + 1) * MEM_HD)
        s = lax.dot_general(q_ref[:, sl], k_ref[:, sl], (((1,), (1,)), ((), ())),
                            preferred_element_type=F32) * (MEM_HD ** -0.5)
        m = jnp.max(s, axis=-1, keepdims=True)
        p = jnp.exp(s - m)
        denom = jnp.sum(p, axis=-1, keepdims=True)
        o = jnp.dot(p.astype(BF16), v_ref[:, sl], preferred_element_type=F32) / denom
        o_ref[:, sl] = o.astype(o_ref.dtype)


def _mem_attn(proj, kv, batch, seq):
    nt = seq // MEM_TILE
    return pl.pallas_call(
        _mem_attn_kernel,
        grid=(batch, nt),
        in_specs=[pl.BlockSpec((MEM_TILE, BRANCH), lambda b, n: (b * nt + n, COL_QM // BRANCH)),
                  pl.BlockSpec((N_MEM, BRANCH), lambda b, n: (b, 0)),
                  pl.BlockSpec((N_MEM, BRANCH), lambda b, n: (b, 1))],
        out_specs=pl.BlockSpec((MEM_TILE, BRANCH), lambda b, n: (b * nt + n, 0)),
        out_shape=jax.ShapeDtypeStruct((batch * seq, BRANCH), BF16),
        compiler_params=_cparams(("parallel", "parallel")),
        name="mem_attn",
    )(proj, kv, kv)


def _route(logits):
    lane_f = lax.broadcasted_iota(I32, logits.shape, 1).astype(F32)
    big = jnp.float32(1e9)
    low = jnp.float32(-3e38)
    gmask = lane_f < N_GROUPS
    gmax = jnp.max(jnp.where(gmask, logits, low), axis=-1, keepdims=True)
    grp = jnp.min(jnp.where(gmask & (logits == gmax), lane_f, big), axis=-1, keepdims=True)
    p_grp = 1.0 / jnp.sum(jnp.where(gmask, jnp.exp(logits - gmax), 0.0), axis=-1, keepdims=True)
    lo = N_GROUPS + grp * EPG
    emask = (lane_f >= lo) & (lane_f < lo + EPG)
    e1 = jnp.max(jnp.where(emask, logits, low), axis=-1, keepdims=True)
    i1 = jnp.min(jnp.where(emask & (logits == e1), lane_f, big), axis=-1, keepdims=True)
    rest = emask & (lane_f != i1)
    e2 = jnp.max(jnp.where(rest, logits, low), axis=-1, keepdims=True)
    i2 = jnp.min(jnp.where(rest & (logits == e2), lane_f, big), axis=-1, keepdims=True)
    t = jnp.exp(e2 - e1)
    w0 = p_grp / (1.0 + t)
    w1 = p_grp * t / (1.0 + t)
    return i1 - N_GROUPS, i2 - N_GROUPS, w0, w1


def _merge_kernel(oa_ref, oc_ref, om_ref, g0_ref, g1_ref, g2_ref, x_ref, wb_ref, wo_ref,
                  lg_ref, lb_ref, wrh_ref, wrl_ref, br_ref,
                  x1_ref, x1p_ref, ids_ref, wts_ref, cnt_ref, run_ref):
    tm = x_ref.shape[0]
    merged = jax.nn.sigmoid(g0_ref[...].astype(F32)) * jnp.dot(
        oa_ref[...], wb_ref[0], preferred_element_type=F32)
    merged = merged + jax.nn.sigmoid(g1_ref[...].astype(F32)) * jnp.dot(
        oc_ref[...], wb_ref[1], preferred_element_type=F32)
    merged = merged + jax.nn.sigmoid(g2_ref[...].astype(F32)) * jnp.dot(
        om_ref[...], wb_ref[2], preferred_element_type=F32)
    y = jnp.dot(merged.astype(BF16), wo_ref[...], preferred_element_type=F32)
    x1 = _layer_norm_rows(ALPHA * x_ref[...] + y, lg_ref[...], lb_ref[...])
    x1_ref[...] = x1
    x1p_ref[...] = _pack_halves(x1)
    xh = x1.astype(BF16)
    xl = (x1 - xh.astype(F32)).astype(BF16)
    logits = (jnp.dot(xh, wrh_ref[...], preferred_element_type=F32)
              + jnp.dot(xl, wrh_ref[...], preferred_element_type=F32)
              + jnp.dot(xh, wrl_ref[...], preferred_element_type=F32)) + br_ref[...]
    e0, e1, w0, w1 = _route(logits)

    @pl.when(pl.program_id(0) == 0)
    def _():
        run_ref[...] = jnp.zeros_like(run_ref)

    lane = lax.broadcasted_iota(I32, (tm, LANES), 1)
    lane_f = lane.astype(F32)
    hot0 = (lane_f == e0).astype(F32)
    hot1 = (lane_f == e1).astype(F32)
    both = hot0 + hot1
    tri = (lax.broadcasted_iota(I32, (tm, tm), 0) > lax.broadcasted_iota(I32, (tm, tm), 1)).astype(BF16)
    before = jnp.dot(tri, both.astype(BF16), preferred_element_type=F32) + run_ref[...]
    r0 = jnp.sum(hot0 * before, axis=-1, keepdims=True)
    r1 = jnp.sum(hot1 * before, axis=-1, keepdims=True)
    run_ref[...] = run_ref[...] + jnp.sum(both, axis=0, keepdims=True)
    cnt_ref[...] = run_ref[...]

    ids = jnp.where(lane == 0, e0, jnp.where(lane == 1, e1, jnp.where(lane == 2, r0, jnp.where(lane == 3, r1, 0.0))))
    ids_ref[...] = ids.astype(I32)
    wts_ref[...] = jnp.where(lane == 0, w0, jnp.where(lane == 1, w1, 0.0))


def _merge(o_attn, o_conv, o_mem, proj, x, w_branch, w_out, ln_g, ln_b, wr_hi, wr_lo, b_r):
    t = x.shape[0]
    tm = TOKEN_TILE
    gcol = COL_GATES // D_MODEL
    row = lambda w: pl.BlockSpec((tm, w), lambda i: (i, 0))
    const = lambda shape: pl.BlockSpec(shape, lambda i: (0,) * len(shape), pipeline_mode=pl.Buffered(1))
    return pl.pallas_call(
        _merge_kernel,
        grid=(t // tm,),
        in_specs=[row(BRANCH), row(BRANCH), row(BRANCH),
                  pl.BlockSpec((tm, D_MODEL), lambda i: (i, gcol)),
                  pl.BlockSpec((tm, D_MODEL), lambda i: (i, gcol + 1)),
                  pl.BlockSpec((tm, D_MODEL), lambda i: (i, gcol + 2)),
                  row(D_MODEL),
                  const((3, BRANCH, D_MODEL)), const((D_MODEL, D_MODEL)),
                  const((1, D_MODEL)), const((1, D_MODEL)),
                  const((D_MODEL, LANES)), const((D_MODEL, LANES)), const((1, LANES))],
        out_specs=[row(D_MODEL), row(D_MODEL // 2), row(LANES), row(LANES),
                   pl.BlockSpec((1, LANES), lambda i: (0, 0))],
        out_shape=[jax.ShapeDtypeStruct((t, D_MODEL), F32),
                   jax.ShapeDtypeStruct((t, D_MODEL // 2), U32),
                   jax.ShapeDtypeStruct((t, LANES), I32),
                   jax.ShapeDtypeStruct((t, LANES), F32),
                   jax.ShapeDtypeStruct((1, LANES), F32)],
        scratch_shapes=[pltpu.VMEM((1, LANES), F32)],
        compiler_params=_cparams(("arbitrary",)),
        name="merge",
    )(o_attn, o_conv, o_mem, proj, proj, proj, x, w_branch, w_out,
      ln_g.reshape(1, D_MODEL), ln_b.reshape(1, D_MODEL), wr_hi, wr_lo, b_r)


def _row_copy(src, src_row, dst, dst_row, sem):
    return pltpu.make_async_copy(src.at[pl.ds(src_row, 1), :], dst.at[pl.ds(dst_row, 1), :], sem)


def _dispatch_kernel(pad_start_ref, pad_len_ref, nu_ref, pos_ref, x_ref, xs_ref, z_ref, sem, zsem):
    tm = x_ref.shape[0]

    def start(r, carry):
        for k in range(2):
            _row_copy(x_ref, r, xs_ref, pos_ref[0, 0, 2 * r + k], sem).start()
        return carry

    lax.fori_loop(0, tm, start, 0, unroll=8)

    @pl.when(pl.program_id(0) == pl.num_programs(0) - 1)
    def _():
        z_ref[...] = jnp.zeros_like(z_ref)

        def per_expert(e, carry):
            base = pad_start_ref[e]
            n_pad = pad_len_ref[e]

            def zstart(r, c):
                _row_copy(z_ref, 0, xs_ref, base + r, zsem).start()
                return c

            def zwait(r, c):
                _row_copy(z_ref, 0, xs_ref, 0, zsem).wait()
                return c

            lax.fori_loop(0, n_pad, zstart, 0)
            lax.fori_loop(0, n_pad, zwait, 0)
            return carry

        lax.fori_loop(0, N_EXPERTS, per_expert, 0)

        def tile_copy(tile):
            row = pl.multiple_of(tile * EXPERT_TILE, EXPERT_TILE)
            return pltpu.make_async_copy(z_ref, xs_ref.at[pl.ds(row, EXPERT_TILE), :], zsem)

        def tstart(tile, c):
            tile_copy(tile).start()
            return c

        def twait(tile, c):
            tile_copy(tile).wait()
            return c

        n_tiles = xs_ref.shape[0] // EXPERT_TILE
        lax.fori_loop(nu_ref[0], n_tiles, tstart, 0)
        lax.fori_loop(nu_ref[0], n_tiles, twait, 0)

    for _ in range(2 * tm):
        _row_copy(x_ref, 0, xs_ref, 0, sem).wait()


def _dispatch(x1p, pos, pad_start, pad_len, n_used, n_slots):
    t, w = x1p.shape
    tm = TOKEN_TILE
    grid_spec = pltpu.PrefetchScalarGridSpec(
        num_scalar_prefetch=3,
        grid=(t // tm,),
        in_specs=[pl.BlockSpec((1, 1, 2 * tm), lambda i, ps, pn, nu: (i, 0, 0), memory_space=pltpu.SMEM),
                  pl.BlockSpec((tm, w), lambda i, ps, pn, nu: (i, 0))],
        out_specs=pl.BlockSpec(memory_space=pl.ANY),
        scratch_shapes=[pltpu.VMEM((EXPERT_TILE, w), U32), pltpu.SemaphoreType.DMA(()),
                        pltpu.SemaphoreType.DMA(())],
    )
    return pl.pallas_call(
        _dispatch_kernel,
        grid_spec=grid_spec,
        out_shape=jax.ShapeDtypeStruct((n_slots, w), U32),
        compiler_params=pltpu.CompilerParams(dimension_semantics=("arbitrary",),
                                             vmem_limit_bytes=VMEM_LIMIT, has_side_effects=True),
        name="dispatch",
    )(pad_start, pad_len, n_used, pos.reshape(t // tm, 1, 2 * tm), x1p)


def _expert_kernel(te_ref, nu_ref, xs_ref, w1_ref, w3_ref, w2_ref, ys_ref, w1b_ref, w3b_ref, w2b_ref):
    i = pl.program_id(0)
    used = i < nu_ref[0]
    new_expert = (i == 0) | (te_ref[i] != te_ref[jnp.maximum(i - 1, 0)])

    @pl.when(new_expert)
    def _():
        w1b_ref[...] = w1_ref[...].astype(BF16)
        w3b_ref[...] = w3_ref[...].astype(BF16)
        w2b_ref[...] = w2_ref[...].astype(BF16)

    @pl.when(used)
    def _():
        half = D_MODEL // 2
        lo, hi = _unpack_halves(xs_ref[...])
        lo = lo.astype(BF16)
        hi = hi.astype(BF16)
        a = (jnp.dot(lo, w1b_ref[:half, :], preferred_element_type=F32)
             + jnp.dot(hi, w1b_ref[half:, :], preferred_element_type=F32))
        b = (jnp.dot(lo, w3b_ref[:half, :], preferred_element_type=F32)
             + jnp.dot(hi, w3b_ref[half:, :], preferred_element_type=F32))
        h = (a * jax.nn.sigmoid(a)) * b
        y = jnp.dot(h.astype(BF16), w2b_ref[...], preferred_element_type=F32)
        ys_ref[...] = _pack_halves(y)

    @pl.when(jnp.logical_not(used))
    def _():
        ys_ref[...] = jnp.zeros_like(ys_ref)


def _experts(xs, tile_expert, n_used, w1, w3, w2, layer):
    n_slots, w = xs.shape
    te = EXPERT_TILE
    wspec = lambda a, b: pl.BlockSpec((None, None, a, b), lambda i, te_r, nu_r: (layer, te_r[i], 0, 0))
    grid_spec = pltpu.PrefetchScalarGridSpec(
        num_scalar_prefetch=2,
        grid=(n_slots // te,),
        in_specs=[pl.BlockSpec((te, w), lambda i, te_r, nu_r: (jnp.minimum(i, nu_r[0] - 1), 0)),
                  wspec(D_MODEL, D_EXPERT), wspec(D_MODEL, D_EXPERT), wspec(D_EXPERT, D_MODEL)],
        out_specs=pl.BlockSpec((te, w), lambda i, te_r, nu_r: (i, 0)),
        scratch_shapes=[pltpu.VMEM((D_MODEL, D_EXPERT), BF16), pltpu.VMEM((D_MODEL, D_EXPERT), BF16),
                        pltpu.VMEM((D_EXPERT, D_MODEL), BF16)],
    )
    return pl.pallas_call(
        _expert_kernel,
        grid_spec=grid_spec,
        out_shape=jax.ShapeDtypeStruct((n_slots, w), U32),
        compiler_params=_cparams(("arbitrary",)),
        name="experts",
    )(tile_expert, n_used, xs, w1, w3, w2)


def _combine_kernel(pos_ref, posn_ref, wts_ref, x1_ref, lg_ref, lb_ref, ys_ref,
                    x2_ref, x2b_ref, buf_ref, sem):
    i = pl.program_id(0)
    n = pl.num_programs(0)
    tm = x1_ref.shape[0]
    slot = i % 2

    def start(p_ref, s, r):
        for k in range(2):
            _row_copy(ys_ref, p_ref[0, 0, 2 * r + k], buf_ref.at[s, k], r, sem.at[s]).start()

    def drain(s):
        for _ in range(2 * tm):
            _row_copy(ys_ref, 0, buf_ref.at[s, 0], 0, sem.at[s]).wait()

    @pl.when(i == 0)
    def _():
        def body(r, carry):
            start(pos_ref, 0, r)
            return carry
        lax.fori_loop(0, tm, body, 0, unroll=8)

    drain(slot)
    for r in range(tm):
        start(posn_ref, 1 - slot, r)

    lo0, hi0 = _unpack_halves(buf_ref[slot, 0])
    lo1, hi1 = _unpack_halves(buf_ref[slot, 1])
    w0 = wts_ref[:, 0:1]
    w1 = wts_ref[:, 1:2]
    f = jnp.concatenate([w0 * lo0 + w1 * lo1, w0 * hi0 + w1 * hi1], axis=1)
    x2 = _layer_norm_rows(ALPHA * x1_ref[...] + f, lg_ref[...], lb_ref[...])
    x2_ref[...] = x2
    x2b_ref[...] = x2.astype(BF16)

    @pl.when(i == n - 1)
    def _():
        drain(1 - slot)


def _combine(ys, pos, wts, x1, ln_g, ln_b):
    t = x1.shape[0]
    tm = TOKEN_TILE
    nt = t // tm
    w = ys.shape[1]
    pos3 = pos.reshape(nt, 1, 2 * tm)
    row = lambda width: pl.BlockSpec((tm, width), lambda i: (i, 0))
    vec = lambda: pl.BlockSpec((1, D_MODEL), lambda i: (0, 0))
    return pl.pallas_call(
        _combine_kernel,
        grid=(nt,),
        in_specs=[pl.BlockSpec((1, 1, 2 * tm), lambda i: (i, 0, 0), memory_space=pltpu.SMEM),
                  pl.BlockSpec((1, 1, 2 * tm), lambda i: (jnp.minimum(i + 1, nt - 1), 0, 0),
                               memory_space=pltpu.SMEM),
                  row(LANES), row(D_MODEL), vec(), vec(),
                  pl.BlockSpec(memory_space=pl.ANY)],
        out_specs=[row(D_MODEL), row(D_MODEL)],
        out_shape=[jax.ShapeDtypeStruct((t, D_MODEL), F32),
                   jax.ShapeDtypeStruct((t, D_MODEL), BF16)],
        scratch_shapes=[pltpu.VMEM((2, 2, tm, w), U32), pltpu.SemaphoreType.DMA((2,))],
        compiler_params=_cparams(("arbitrary",)),
        name="combine",
    )(pos3, pos3, wts, x1, ln_g.reshape(1, D_MODEL), ln_b.reshape(1, D_MODEL), ys)


def _dispatch_plan(ids, counts_f, n_tiles):
    counts = counts_f[0, :N_EXPERTS].astype(I32)
    padded = ((counts + EXPERT_TILE - 1) // EXPERT_TILE) * EXPERT_TILE
    ends = jnp.cumsum(padded)
    starts = ends - padded
    hot = ids[:, 0:2, None] == jnp.arange(N_EXPERTS, dtype=I32)[None, None, :]
    pos = (ids[:, 2:4] + jnp.sum(jnp.where(hot, starts[None, None, :], 0), axis=-1)).reshape(-1)
    n_used = (ends[-1] // EXPERT_TILE).astype(I32)
    tile_start = jnp.arange(n_tiles, dtype=I32) * EXPERT_TILE
    tile_e = jnp.minimum(jnp.sum((ends[None, :] <= tile_start[:, None]).astype(I32), axis=1),
                         N_EXPERTS - 1)
    last_e = jnp.sum(jnp.where(jnp.arange(n_tiles) == n_used - 1, tile_e, 0))
    tile_e = jnp.where(jnp.arange(n_tiles) < n_used, tile_e, last_e).astype(I32)
    return pos, starts + counts, padded - counts, tile_e, n_used.reshape(1)


def kernel(x, mem, mem_ln_g, mem_ln_b, w_in, attn_sinks, conv_dw, conv_dw_b, conv_ln_g, conv_ln_b,
           w_mem_kv, w_branch, w_out, ln1_g, ln1_b, router_group, router_group_b, router_expert,
           router_expert_b, w1, w3, w2, ln2_g, ln2_b):
    batch, seq, d = x.shape
    depth = w_in.shape[0]
    t = batch * seq
    n_slots = 2 * t + N_EXPERTS * EXPERT_TILE
    n_tiles = n_slots // EXPERT_TILE

    xf = x.reshape(t, d)
    xb = xf.astype(BF16)
    mem_n = _ln_rows(mem.reshape(batch * N_MEM, d), mem_ln_g, mem_ln_b, BF16, 512)

    for l in range(depth):
        proj = _project(xb, w_in, l, _in_proj_col_block, min(IN_TILE_M, t), IN_TILE_N, "in_proj")
        o_attn = _swa(proj, _swa_bias_mask(attn_sinks[l]), batch, seq)
        o_conv = _conv_branch(proj, conv_dw[l], conv_dw_b[l], conv_ln_g[l], conv_ln_b[l], batch, seq)
        kv = _project(mem_n, w_mem_kv, l, lambda j: j, 512, 512, "mem_kv")
        o_mem = _mem_attn(proj, kv, batch, seq)

        w_r = jnp.zeros((d, LANES), F32)
        w_r = w_r.at[:, :N_GROUPS].set(router_group[l]).at[:, N_GROUPS:N_GROUPS + N_EXPERTS].set(
            router_expert[l])
        b_r = jnp.zeros((1, LANES), F32)
        b_r = b_r.at[0, :N_GROUPS].set(router_group_b[l]).at[0, N_GROUPS:N_GROUPS + N_EXPERTS].set(
            router_expert_b[l])
        wr_hi = w_r.astype(BF16)
        wr_lo = (w_r - wr_hi.astype(F32)).astype(BF16)
        x1, x1p, ids, wts, counts = _merge(o_attn, o_conv, o_mem, proj, xf, w_branch[l].astype(BF16),
                                           w_out[l].astype(BF16), ln1_g[l], ln1_b[l], wr_hi, wr_lo, b_r)

        pos, pad_start, pad_len, tile_e, n_used = _dispatch_plan(ids, counts, n_tiles)
        xs = _dispatch(x1p, pos, pad_start, pad_len, n_used, n_slots)
        ys = _experts(xs, tile_e, n_used, w1, w3, w2, l)
        xf, xb = _combine(ys, pos, wts, x1, ln2_g[l], ln2_b[l])

    return xf.reshape(batch, seq, d)
```

```python
import jax
import jax.numpy as jnp
from jax import lax
from jax.experimental import pallas as pl
from jax.experimental.pallas import tpu as pltpu

F32 = jnp.float32
BF16 = jnp.bfloat16
U32 = jnp.uint32
I32 = jnp.int32

D_MODEL = 2048
BRANCH = 1024
HEAD_DIM = 64
N_Q_HEADS = 16
N_KV_HEADS = 4
GQA = 4
WINDOW = 128
BLOCK = 128
KV_WIDTH = N_KV_HEADS * HEAD_DIM
CONV_K = 31
CONV_HIST = 32
N_MEM = 256
MEM_HEADS = 4
MEM_HD = 256
N_GROUPS = 4
EPG = 8
N_EXPERTS = 32
D_EXPERT = 512
ALPHA = (2.0 * 4) ** 0.25
LN_EPS = 1e-5
NEG_INF = -1e30
LOG2E = 1.4426950408889634

LANES = 128
SUBLANES = 8

COL_Q, COL_A, COL_G, COL_QM, COL_GATES, COL_K, COL_V = 0, 1024, 2048, 3072, 4096, 10240, 10496
IN_TILE_N = 512
IN_TILE_M = 2048

EXPERT_TILE = 256
TOKEN_TILE = 256
VMEM_LIMIT = 56 * 1024 * 1024


def _cparams(sem, vmem=VMEM_LIMIT):
    return pltpu.CompilerParams(dimension_semantics=sem, vmem_limit_bytes=vmem)


def _layer_norm_rows(z, g, b):
    mu = jnp.mean(z, axis=-1, keepdims=True)
    c = z - mu
    var = jnp.mean(c * c, axis=-1, keepdims=True)
    return c * lax.rsqrt(var + LN_EPS) * g + b


def _pack_halves(x):
    n = x.shape[1] // 2
    lo = lax.bitcast_convert_type(x[:, :n].astype(BF16).astype(F32), U32)
    hi = lax.bitcast_convert_type(x[:, n:].astype(BF16).astype(F32), U32)
    return (lo >> 16) | hi


def _unpack_halves(u):
    lo = lax.bitcast_convert_type(u << 16, F32)
    hi = lax.bitcast_convert_type(u & jnp.uint32(0xFFFF0000), F32)
    return lo, hi


def _proj_kernel(x_ref, w_ref, o_ref):
    o_ref[...] = jnp.dot(x_ref[...], w_ref[...].astype(BF16),
                         preferred_element_type=F32).astype(o_ref.dtype)


def _project(x, w_all, layer, col_block, tm, tn, name):
    m, k = x.shape
    n = w_all.shape[2]
    return pl.pallas_call(
        _proj_kernel,
        grid=(m // tm, n // tn),
        in_specs=[pl.BlockSpec((tm, k), lambda i, j: (i, 0)),
                  pl.BlockSpec((None, k, tn), lambda i, j: (layer, 0, col_block(j)))],
        out_specs=pl.BlockSpec((tm, tn), lambda i, j: (i, j)),
        out_shape=jax.ShapeDtypeStruct((m, n), BF16),
        compiler_params=_cparams(("parallel", "parallel")),
        name=name,
    )(x, w_all)


def _in_proj_col_block(j):
    kv_src = 1024 // IN_TILE_N
    n_tiles = 10752 // IN_TILE_N
    return jnp.where(j < kv_src, j, jnp.where(j < n_tiles - 1, j + 1, kv_src))


def _ln_kernel(x_ref, g_ref, b_ref, o_ref):
    o_ref[...] = _layer_norm_rows(x_ref[...], g_ref[...], b_ref[...]).astype(o_ref.dtype)


def _ln_rows(x, g, b, out_dtype, tm):
    m, d = x.shape
    return pl.pallas_call(
        _ln_kernel,
        grid=(m // tm,),
        in_specs=[pl.BlockSpec((tm, d), lambda i: (i, 0)),
                  pl.BlockSpec((1, d), lambda i: (0, 0)),
                  pl.BlockSpec((1, d), lambda i: (0, 0))],
        out_specs=pl.BlockSpec((tm, d), lambda i: (i, 0)),
        out_shape=jax.ShapeDtypeStruct((m, d), out_dtype),
        compiler_params=_cparams(("parallel",)),
        name="mem_ln",
    )(x, g.reshape(1, d), b.reshape(1, d))


def _swap_halves(t):
    u = pltpu.bitcast(t, U32)
    return pltpu.bitcast(pltpu.roll(u, LANES // 2, axis=1), BF16)


def _swa_kernel(q_ref, kc_ref, kp_ref, vc_ref, vp_ref, bm_ref, o_ref):
    low = lax.broadcasted_iota(I32, (1, LANES), 1) < HEAD_DIM
    sink_row = lax.broadcasted_iota(I32, (2 * BLOCK, 1), 0) == 0
    zero = jnp.zeros((), BF16)
    scale = (HEAD_DIM ** -0.5) * LOG2E
    for kv in range(N_KV_HEADS):
        tile = slice((kv // 2) * LANES, (kv // 2 + 1) * LANES)
        kt = jnp.concatenate([kp_ref[:, tile], kc_ref[:, tile]], axis=0)
        vt = jnp.concatenate([vp_ref[:, tile], vc_ref[:, tile]], axis=0)
        kt = jnp.where(sink_row, zero, kt)
        vt = jnp.where(sink_row, zero, vt)
        ks, vs = _swap_halves(kt), _swap_halves(vt)
        k_lo, k_hi = (kt, ks) if kv % 2 == 0 else (ks, kt)
        v_lo, v_hi = (vt, vs) if kv % 2 == 0 else (vs, vt)
        kk = jnp.where(low, k_lo, k_hi)
        vv = (jnp.where(low, v_lo, zero), jnp.where(low, zero, v_hi))
        qts = [q_ref[:, (2 * kv + t) * LANES:(2 * kv + t + 1) * LANES] for t in range(2)]
        q4 = jnp.concatenate([jnp.where(low, qts[0], zero), jnp.where(low, zero, qts[0]),
                              jnp.where(low, qts[1], zero), jnp.where(low, zero, qts[1])], axis=0)
        s4 = lax.dot_general(q4, kk, (((1,), (1,)), ((), ())), preferred_element_type=F32)
        s4 = s4 * scale + bm_ref[0, kv]
        for t in range(2):
            acc = None
            for half in range(2):
                g = 2 * t + half
                s = s4[g * BLOCK:(g + 1) * BLOCK]
                p = jnp.exp2(s - jnp.max(s, axis=-1, keepdims=True))
                denom = jnp.sum(p, axis=-1, keepdims=True)
                o = jnp.dot(p.astype(BF16), vv[half], preferred_element_type=F32) / denom
                acc = o if acc is None else acc + o
            o_ref[:, (2 * kv + t) * LANES:(2 * kv + t + 1) * LANES] = acc.astype(o_ref.dtype)


def _swa_bias_mask(sinks):
    qi = jnp.arange(BLOCK)[:, None] + BLOCK
    si = jnp.arange(2 * BLOCK)[None, :]
    dist = qi - si
    valid = (dist >= 0) & (dist < WINDOW)
    hh = jnp.arange(1, N_Q_HEADS + 1, dtype=F32)
    slopes = jnp.exp2(-8.0 * hh / N_Q_HEADS)
    bias = -slopes[:, None, None] * dist.astype(F32)[None]
    later = jnp.where(valid[None], bias, NEG_INF)
    first = jnp.where((valid & (si >= BLOCK))[None], bias, NEG_INF)
    both = jnp.stack([first, later])
    both = jnp.where((si == 0)[None, None], sinks.astype(F32)[None, :, None, None], both) * LOG2E
    return both.reshape(2, N_KV_HEADS, GQA * BLOCK, 2 * BLOCK)


def _swa(proj, bias_mask, batch, seq):
    nb = seq // BLOCK
    kcol, vcol = COL_K // KV_WIDTH, COL_V // KV_WIDTH

    def cur(b, n):
        return b * nb + n

    def prev(b, n):
        return b * nb + jnp.maximum(n - 1, 0)

    return pl.pallas_call(
        _swa_kernel,
        grid=(batch, nb),
        in_specs=[pl.BlockSpec((BLOCK, BRANCH), lambda b, n: (cur(b, n), COL_Q // BRANCH)),
                  pl.BlockSpec((BLOCK, KV_WIDTH), lambda b, n: (cur(b, n), kcol)),
                  pl.BlockSpec((BLOCK, KV_WIDTH), lambda b, n: (prev(b, n), kcol)),
                  pl.BlockSpec((BLOCK, KV_WIDTH), lambda b, n: (cur(b, n), vcol)),
                  pl.BlockSpec((BLOCK, KV_WIDTH), lambda b, n: (prev(b, n), vcol)),
                  pl.BlockSpec((1, N_KV_HEADS, GQA * BLOCK, 2 * BLOCK),
                               lambda b, n: (jnp.minimum(n, 1), 0, 0, 0))],
        out_specs=pl.BlockSpec((BLOCK, BRANCH), lambda b, n: (cur(b, n), 0)),
        out_shape=jax.ShapeDtypeStruct((batch * seq, BRANCH), BF16),
        compiler_params=_cparams(("parallel", "arbitrary")),
        name="swa",
    )(proj, proj, proj, proj, proj, bias_mask)


CONV_TILE = 256
CONV_CHUNK = 64
CONV_ROWS = CONV_HIST + CONV_TILE


def _conv_kernel(ac_ref, gc_ref, ah_ref, gh_ref, w_ref, b_ref, lg_ref, lb_ref, o_ref, h_ref, c_ref):
    first = pl.program_id(1) == 0
    hist = ah_ref[...].astype(F32) * jax.nn.sigmoid(gh_ref[...].astype(F32))
    h_ref[0, 0:CONV_HIST, :] = jnp.where(first, 0.0, hist)
    h_ref[0, CONV_HIST:, :] = ac_ref[...].astype(F32) * jax.nn.sigmoid(gc_ref[...].astype(F32))
    span = CONV_ROWS - SUBLANES
    for s in range(1, SUBLANES):
        h_ref[s, 0:span, :] = h_ref[0, s:s + span, :]
    base = CONV_HIST - (CONV_K - 1)
    groups = CONV_CHUNK // SUBLANES
    for lt in range(BRANCH // LANES):
        cols = slice(lt * LANES, (lt + 1) * LANES)
        taps = [w_ref[j * SUBLANES:(j + 1) * SUBLANES, cols] for j in range(CONV_K)]
        bias = jnp.broadcast_to(b_ref[:, cols], (groups, SUBLANES, LANES))

        def chunk(i, carry, cols=cols, taps=taps, bias=bias):
            r0 = pl.multiple_of(i * CONV_CHUNK, CONV_CHUNK)
            acc = bias
            for s in range(SUBLANES):
                offs = [off for off in range(base, base + CONV_K) if off % SUBLANES == s]
                q0, q1 = offs[0] // SUBLANES, offs[-1] // SUBLANES
                rows = (q1 - q0) * SUBLANES + CONV_CHUNK
                win = h_ref[s, pl.ds(r0 + q0 * SUBLANES, rows), cols]
                win = win.reshape(rows // SUBLANES, SUBLANES, LANES)
                for off in offs:
                    q = off // SUBLANES - q0
                    acc = acc + taps[off - base][None] * win[q:q + groups]
            c_ref[pl.ds(r0, CONV_CHUNK), cols] = acc.reshape(CONV_CHUNK, LANES)
            return carry

        lax.fori_loop(0, CONV_TILE // CONV_CHUNK, chunk, 0)
    y = _layer_norm_rows(c_ref[...], lg_ref[...], lb_ref[...])
    o_ref[...] = (y * jax.nn.sigmoid(y)).astype(o_ref.dtype)


def _conv_branch(proj, w_dw, b_dw, ln_g, ln_b, batch, seq):
    nt = seq // CONV_TILE
    hist_per_tile = CONV_TILE // CONV_HIST
    w_rep = jnp.repeat(w_dw, SUBLANES, axis=0)

    def cur(b, n):
        return b * nt + n

    def hist(b, n):
        return jnp.maximum((b * nt + n) * hist_per_tile - 1, 0)

    vec = lambda: pl.BlockSpec((1, BRANCH), lambda b, n: (0, 0))
    return pl.pallas_call(
        _conv_kernel,
        grid=(batch, nt),
        in_specs=[pl.BlockSpec((CONV_TILE, BRANCH), lambda b, n: (cur(b, n), COL_A // BRANCH)),
                  pl.BlockSpec((CONV_TILE, BRANCH), lambda b, n: (cur(b, n), COL_G // BRANCH)),
                  pl.BlockSpec((CONV_HIST, BRANCH), lambda b, n: (hist(b, n), COL_A // BRANCH)),
                  pl.BlockSpec((CONV_HIST, BRANCH), lambda b, n: (hist(b, n), COL_G // BRANCH)),
                  pl.BlockSpec((CONV_K * SUBLANES, BRANCH), lambda b, n: (0, 0)),
                  vec(), vec(), vec()],
        out_specs=pl.BlockSpec((CONV_TILE, BRANCH), lambda b, n: (cur(b, n), 0)),
        out_shape=jax.ShapeDtypeStruct((batch * seq, BRANCH), BF16),
        scratch_shapes=[pltpu.VMEM((SUBLANES, CONV_ROWS, BRANCH), F32),
                        pltpu.VMEM((CONV_TILE, BRANCH), F32)],
        compiler_params=_cparams(("parallel", "arbitrary")),
        name="conv",
    )(proj, proj, proj, proj, w_rep, b_dw.reshape(1, BRANCH), ln_g.reshape(1, BRANCH),
      ln_b.reshape(1, BRANCH))


MEM_TILE = 512


def _mem_attn_kernel(q_ref, k_ref, v_ref, o_ref):
    for h in range(MEM_HEADS):
        sl = slice(h * MEM_HD, (h + 1) * MEM_HD)
        s = lax.dot_general(q_ref[:, sl], k_ref[:, sl], (((1,), (1,)), ((), ())),
                            preferred_element_type=F32) * (MEM_HD ** -0.5)
        m = jnp.max(s, axis=-1, keepdims=True)
        p = jnp.exp(s - m)
        denom = jnp.sum(p, axis=-1, keepdims=True)
        o = jnp.dot(p.astype(BF16), v_ref[:, sl], preferred_element_type=F32) / denom
        o_ref[:, sl] = o.astype(o_ref.dtype)


def _mem_attn(proj, kv, batch, seq):
    nt = seq // MEM_TILE
    return pl.pallas_call(
        _mem_attn_kernel,
        grid=(batch, nt),
        in_specs=[pl.BlockSpec((MEM_TILE, BRANCH), lambda b, n: (b * nt + n, COL_QM // BRANCH)),
                  pl.BlockSpec((N_MEM, BRANCH), lambda b, n: (b, 0)),
                  pl.BlockSpec((N_MEM, BRANCH), lambda b, n: (b, 1))],
        out_specs=pl.BlockSpec((MEM_TILE, BRANCH), lambda b, n: (b * nt + n, 0)),
        out_shape=jax.ShapeDtypeStruct((batch * seq, BRANCH), BF16),
        compiler_params=_cparams(("parallel", "parallel")),
        name="mem_attn",
    )(proj, kv, kv)


def _route(logits):
    lane_f = lax.broadcasted_iota(I32, logits.shape, 1).astype(F32)
    big = jnp.float32(1e9)
    low = jnp.float32(-3e38)
    gmask = lane_f < N_GROUPS
    gmax = jnp.max(jnp.where(gmask, logits, low), axis=-1, keepdims=True)
    grp = jnp.min(jnp.where(gmask & (logits == gmax), lane_f, big), axis=-1, keepdims=True)
    p_grp = 1.0 / jnp.sum(jnp.where(gmask, jnp.exp(logits - gmax), 0.0), axis=-1, keepdims=True)
    lo = N_GROUPS + grp * EPG
    emask = (lane_f >= lo) & (lane_f < lo + EPG)
    e1 = jnp.max(jnp.where(emask, logits, low), axis=-1, keepdims=True)
    i1 = jnp.min(jnp.where(emask & (logits == e1), lane_f, big), axis=-1, keepdims=True)
    rest = emask & (lane_f != i1)
    e2 = jnp.max(jnp.where(rest, logits, low), axis=-1, keepdims=True)
    i2 = jnp.min(jnp.where(rest & (logits == e2), lane_f, big), axis=-1, keepdims=True)
    t = jnp.exp(e2 - e1)
    w0 = p_grp / (1.0 + t)
    w1 = p_grp * t / (1.0 + t)
    return i1 - N_GROUPS, i2 - N_GROUPS, w0, w1


def _merge_kernel(oa_ref, oc_ref, om_ref, g0_ref, g1_ref, g2_ref, x_ref, wb_ref, wo_ref,
                  lg_ref, lb_ref, wr_ref, br_ref,
                  x1_ref, x1p_ref, ids_ref, wts_ref, cnt_ref, za_ref, zb_ref, run_ref):
    i = pl.program_id(0)

    @pl.when(i == 0)
    def _():
        zb_ref[...] = jnp.zeros_like(zb_ref)
        run_ref[...] = jnp.zeros_like(run_ref)

    refs = (oa_ref, oc_ref, om_ref, g0_ref, g1_ref, g2_ref, x_ref, wb_ref, wo_ref, lg_ref, lb_ref,
            wr_ref, br_ref, x1_ref, x1p_ref, ids_ref, wts_ref, cnt_ref, run_ref)

    @pl.when(i % 2 == 0)
    def _():
        _merge_step(i, za_ref, zb_ref, *refs)

    @pl.when(i % 2 == 1)
    def _():
        _merge_step(i, zb_ref, za_ref, *refs)


def _merge_step(i, zcur_ref, zprev_ref, oa_ref, oc_ref, om_ref, g0_ref, g1_ref, g2_ref, x_ref, wb_ref,
                wo_ref, lg_ref, lb_ref, wr_ref, br_ref, x1_ref, x1p_ref, ids_ref, wts_ref,
                cnt_ref, run_ref):
    tm = x_ref.shape[0]
    merged = jax.nn.sigmoid(g0_ref[...].astype(F32)) * jnp.dot(
        oa_ref[...], wb_ref[0], preferred_element_type=F32)
    merged = merged + jax.nn.sigmoid(g1_ref[...].astype(F32)) * jnp.dot(
        oc_ref[...], wb_ref[1], preferred_element_type=F32)
    merged = merged + jax.nn.sigmoid(g2_ref[...].astype(F32)) * jnp.dot(
        om_ref[...], wb_ref[2], preferred_element_type=F32)
    y = jnp.dot(merged.astype(BF16), wo_ref[...], preferred_element_type=F32)
    zcur_ref[...] = ALPHA * x_ref[...] + y

    x1 = _layer_norm_rows(zprev_ref[...], lg_ref[...], lb_ref[...])
    x1_ref[...] = x1
    x1p_ref[...] = _pack_halves(x1)
    r = jnp.dot(x1.astype(BF16), wr_ref[...], preferred_element_type=F32)
    logits = (r[:, :LANES] + r[:, LANES:]) + br_ref[...]
    e0, e1, w0, w1 = _route(logits)

    lane = lax.broadcasted_iota(I32, (tm, LANES), 1)
    lane_f = lane.astype(F32)
    hot0 = (lane_f == e0).astype(F32)
    hot1 = (lane_f == e1).astype(F32)
    both = hot0 + hot1
    tri = (lax.broadcasted_iota(I32, (tm, tm), 0) > lax.broadcasted_iota(I32, (tm, tm), 1)).astype(BF16)
    run = run_ref[...]
    before = jnp.dot(tri, both.astype(BF16), preferred_element_type=F32) + run
    r0 = jnp.sum(hot0 * before, axis=-1, keepdims=True)
    r1 = jnp.sum(hot1 * before, axis=-1, keepdims=True)
    run = run + jnp.where(i > 0, jnp.sum(both, axis=0, keepdims=True), 0.0)
    run_ref[...] = run
    cnt_ref[...] = run
    ids = jnp.where(lane == 0, e0, jnp.where(lane == 1, e1, jnp.where(lane == 2, r0, jnp.where(lane == 3, r1, 0.0))))
    ids_ref[...] = ids.astype(I32)
    wts_ref[...] = jnp.where(lane == 0, w0, jnp.where(lane == 1, w1, 0.0))


def _merge(o_attn, o_conv, o_mem, proj, x, w_branch, w_out, ln_g, ln_b, w_r, b_r):
    t = x.shape[0]
    tm = TOKEN_TILE
    nt = t // tm
    gcol = COL_GATES // D_MODEL
    cur = lambda w, c=0: pl.BlockSpec((tm, w), lambda i: (jnp.minimum(i, nt - 1), c))
    behind = lambda w: pl.BlockSpec((tm, w), lambda i: (jnp.maximum(i - 1, 0), 0))
    const = lambda shape: pl.BlockSpec(shape, lambda i: (0,) * len(shape), pipeline_mode=pl.Buffered(1))
    return pl.pallas_call(
        _merge_kernel,
        grid=(nt + 1,),
        in_specs=[cur(BRANCH), cur(BRANCH), cur(BRANCH),
                  cur(D_MODEL, gcol), cur(D_MODEL, gcol + 1), cur(D_MODEL, gcol + 2),
                  cur(D_MODEL),
                  const((3, BRANCH, D_MODEL)), const((D_MODEL, D_MODEL)),
                  const((1, D_MODEL)), const((1, D_MODEL)),
                  const((D_MODEL, 2 * LANES)), const((1, LANES))],
        out_specs=[behind(D_MODEL), behind(D_MODEL // 2), behind(LANES), behind(LANES),
                   pl.BlockSpec((1, LANES), lambda i: (0, 0))],
        out_shape=[jax.ShapeDtypeStruct((t, D_MODEL), F32),
                   jax.ShapeDtypeStruct((t, D_MODEL // 2), U32),
                   jax.ShapeDtypeStruct((t, LANES), I32),
                   jax.ShapeDtypeStruct((t, LANES), F32),
                   jax.ShapeDtypeStruct((1, LANES), F32)],
        scratch_shapes=[pltpu.VMEM((tm, D_MODEL), F32), pltpu.VMEM((tm, D_MODEL), F32),
                        pltpu.VMEM((1, LANES), F32)],
        compiler_params=_cparams(("arbitrary",)),
        name="merge",
    )(o_attn, o_conv, o_mem, proj, proj, proj, x, w_branch, w_out,
      ln_g.reshape(1, D_MODEL), ln_b.reshape(1, D_MODEL), w_r, b_r)


def _row_copy(src, src_row, dst, dst_row, sem):
    return pltpu.make_async_copy(src.at[pl.ds(src_row, 1), :], dst.at[pl.ds(dst_row, 1), :], sem)


def _dispatch_kernel(pad_start_ref, pad_len_ref, nu_ref, pos_ref, x_ref, xs_ref, z_ref, sem, zsem):
    tm = x_ref.shape[0]

    def start(r, carry):
        for k in range(2):
            _row_copy(x_ref, r, xs_ref, pos_ref[0, 0, 2 * r + k], sem).start()
        return carry

    lax.fori_loop(0, tm, start, 0, unroll=8)

    @pl.when(pl.program_id(0) == pl.num_programs(0) - 1)
    def _():
        z_ref[...] = jnp.zeros_like(z_ref)

        def pad_copies(e, act):
            base = pad_start_ref[e]
            n_pad = pad_len_ref[e]
            off = base + n_pad
            for bit in range(EXPERT_TILE.bit_length() - 2, SUBLANES.bit_length() - 2, -1):
                size = 1 << bit
                has = (n_pad >> bit) & 1
                off = off - has * size

                @pl.when(has == 1)
                def _(off=off, size=size):
                    row = pl.multiple_of(off, size)
                    act(pltpu.make_async_copy(z_ref.at[pl.ds(0, size), :],
                                              xs_ref.at[pl.ds(row, size), :], zsem))

            def low_rows(r, c):
                act(_row_copy(z_ref, 0, xs_ref, base + r, zsem))
                return c

            lax.fori_loop(0, n_pad & (SUBLANES - 1), low_rows, 0)

        def start_all(e, carry):
            pad_copies(e, lambda cp: cp.start())
            return carry

        def wait_all(e, carry):
            pad_copies(e, lambda cp: cp.wait())
            return carry

        lax.fori_loop(0, N_EXPERTS, start_all, 0)
        lax.fori_loop(0, N_EXPERTS, wait_all, 0)

        def tile_copy(tile):
            row = pl.multiple_of(tile * EXPERT_TILE, EXPERT_TILE)
            return pltpu.make_async_copy(z_ref, xs_ref.at[pl.ds(row, EXPERT_TILE), :], zsem)

        def tstart(tile, c):
            tile_copy(tile).start()
            return c

        def twait(tile, c):
            tile_copy(tile).wait()
            return c

        n_tiles = xs_ref.shape[0] // EXPERT_TILE
        lax.fori_loop(nu_ref[0], n_tiles, tstart, 0)
        lax.fori_loop(nu_ref[0], n_tiles, twait, 0)

    for _ in range(2 * tm):
        _row_copy(x_ref, 0, xs_ref, 0, sem).wait()


def _dispatch(x1p, pos, pad_start, pad_len, n_used, n_slots):
    t, w = x1p.shape
    tm = TOKEN_TILE
    grid_spec = pltpu.PrefetchScalarGridSpec(
        num_scalar_prefetch=3,
        grid=(t // tm,),
        in_specs=[pl.BlockSpec((1, 1, 2 * tm), lambda i, ps, pn, nu: (i, 0, 0), memory_space=pltpu.SMEM),
                  pl.BlockSpec((tm, w), lambda i, ps, pn, nu: (i, 0))],
        out_specs=pl.BlockSpec(memory_space=pl.ANY),
        scratch_shapes=[pltpu.VMEM((EXPERT_TILE, w), U32), pltpu.SemaphoreType.DMA(()),
                        pltpu.SemaphoreType.DMA(())],
    )
    return pl.pallas_call(
        _dispatch_kernel,
        grid_spec=grid_spec,
        out_shape=jax.ShapeDtypeStruct((n_slots, w), U32),
        compiler_params=pltpu.CompilerParams(dimension_semantics=("arbitrary",),
                                             vmem_limit_bytes=VMEM_LIMIT, has_side_effects=True),
        name="dispatch",
    )(pad_start, pad_len, n_used, pos.reshape(t // tm, 1, 2 * tm), x1p)


def _expert_kernel(te_ref, nu_ref, xs_ref, w1_ref, w3_ref, w2_ref, ys_ref, w1b_ref, w3b_ref, w2b_ref):
    i = pl.program_id(0)
    used = i < nu_ref[0]
    new_expert = (i == 0) | (te_ref[i] != te_ref[jnp.maximum(i - 1, 0)])

    @pl.when(new_expert)
    def _():
        w1b_ref[...] = w1_ref[...].astype(BF16)
        w3b_ref[...] = w3_ref[...].astype(BF16)
        w2b_ref[...] = w2_ref[...].astype(BF16)

    @pl.when(used)
    def _():
        half = D_MODEL // 2
        lo, hi = _unpack_halves(xs_ref[...])
        lo = lo.astype(BF16)
        hi = hi.astype(BF16)
        a = (jnp.dot(lo, w1b_ref[:half, :], preferred_element_type=F32)
             + jnp.dot(hi, w1b_ref[half:, :], preferred_element_type=F32))
        b = (jnp.dot(lo, w3b_ref[:half, :], preferred_element_type=F32)
             + jnp.dot(hi, w3b_ref[half:, :], preferred_element_type=F32))
        h = (a * jax.nn.sigmoid(a)) * b
        y = jnp.dot(h.astype(BF16), w2b_ref[...], preferred_element_type=F32)
        ys_ref[...] = _pack_halves(y)

    @pl.when(jnp.logical_not(used))
    def _():
        ys_ref[...] = jnp.zeros_like(ys_ref)


def _experts(xs, tile_expert, n_used, w1, w3, w2, layer):
    n_slots, w = xs.shape
    te = EXPERT_TILE
    wspec = lambda a, b: pl.BlockSpec((None, None, a, b), lambda i, te_r, nu_r: (layer, te_r[i], 0, 0))
    grid_spec = pltpu.PrefetchScalarGridSpec(
        num_scalar_prefetch=2,
        grid=(n_slots // te,),
        in_specs=[pl.BlockSpec((te, w), lambda i, te_r, nu_r: (jnp.minimum(i, nu_r[0] - 1), 0)),
                  wspec(D_MODEL, D_EXPERT), wspec(D_MODEL, D_EXPERT), wspec(D_EXPERT, D_MODEL)],
        out_specs=pl.BlockSpec((te, w), lambda i, te_r, nu_r: (i, 0)),
        scratch_shapes=[pltpu.VMEM((D_MODEL, D_EXPERT), BF16), pltpu.VMEM((D_MODEL, D_EXPERT), BF16),
                        pltpu.VMEM((D_EXPERT, D_MODEL), BF16)],
    )
    return pl.pallas_call(
        _expert_kernel,
        grid_spec=grid_spec,
        out_shape=jax.ShapeDtypeStruct((n_slots, w), U32),
        compiler_params=_cparams(("arbitrary",)),
        name="experts",
    )(tile_expert, n_used, xs, w1, w3, w2)


def _combine_kernel(pos_ref, posn_ref, wts_ref, x1_ref, lg_ref, lb_ref, ys_ref,
                    x2_ref, x2b_ref, bufa_ref, bufb_ref, sema, semb):
    i = pl.program_id(0)
    n = pl.num_programs(0)
    tm = TOKEN_TILE

    def start(p_ref, base, buf, sem, r):
        for k in range(2):
            _row_copy(ys_ref, p_ref[0, 0, base + 2 * r + k], buf.at[k], r, sem).start()

    def drain(buf, sem):
        for _ in range(2 * tm):
            _row_copy(ys_ref, 0, buf.at[0], 0, sem).wait()

    def finish(buf, rows):
        lo0, hi0 = _unpack_halves(buf[0])
        lo1, hi1 = _unpack_halves(buf[1])
        w0 = wts_ref[rows, 0:1]
        w1 = wts_ref[rows, 1:2]
        f = jnp.concatenate([w0 * lo0 + w1 * lo1, w0 * hi0 + w1 * hi1], axis=1)
        x2 = _layer_norm_rows(ALPHA * x1_ref[rows, :] + f, lg_ref[...], lb_ref[...])
        x2_ref[rows, :] = x2
        x2b_ref[rows, :] = x2.astype(BF16)

    @pl.when(i == 0)
    def _():
        def body(r, carry):
            start(pos_ref, 0, bufa_ref, sema, r)
            return carry
        lax.fori_loop(0, tm, body, 0, unroll=8)

    drain(bufa_ref, sema)
    for r in range(tm):
        start(pos_ref, 2 * tm, bufb_ref, semb, r)
    finish(bufa_ref, slice(0, tm))
    drain(bufb_ref, semb)
    for r in range(tm):
        start(posn_ref, 0, bufa_ref, sema, r)
    finish(bufb_ref, slice(tm, 2 * tm))

    @pl.when(i == n - 1)
    def _():
        drain(bufa_ref, sema)


def _combine(ys, pos, wts, x1, ln_g, ln_b):
    t = x1.shape[0]
    tm = TOKEN_TILE
    ns = t // (2 * tm)
    w = ys.shape[1]
    pos3 = pos.reshape(ns, 1, 4 * tm)
    row = lambda width: pl.BlockSpec((2 * tm, width), lambda i: (i, 0))
    vec = lambda: pl.BlockSpec((1, D_MODEL), lambda i: (0, 0))
    return pl.pallas_call(
        _combine_kernel,
        grid=(ns,),
        in_specs=[pl.BlockSpec((1, 1, 4 * tm), lambda i: (i, 0, 0), memory_space=pltpu.SMEM),
                  pl.BlockSpec((1, 1, 4 * tm), lambda i: (jnp.minimum(i + 1, ns - 1), 0, 0),
                               memory_space=pltpu.SMEM),
                  row(LANES), row(D_MODEL), vec(), vec(),
                  pl.BlockSpec(memory_space=pl.ANY)],
        out_specs=[row(D_MODEL), row(D_MODEL)],
        out_shape=[jax.ShapeDtypeStruct((t, D_MODEL), F32),
                   jax.ShapeDtypeStruct((t, D_MODEL), BF16)],
        scratch_shapes=[pltpu.VMEM((2, tm, w), U32), pltpu.VMEM((2, tm, w), U32),
                        pltpu.SemaphoreType.DMA(()), pltpu.SemaphoreType.DMA(())],
        compiler_params=_cparams(("arbitrary",)),
        name="combine",
    )(pos3, pos3, wts, x1, ln_g.reshape(1, D_MODEL), ln_b.reshape(1, D_MODEL), ys)


def _dispatch_plan(ids, counts_f, n_tiles):
    counts = counts_f[0, :N_EXPERTS].astype(I32)
    padded = ((counts + EXPERT_TILE - 1) // EXPERT_TILE) * EXPERT_TILE
    ends = jnp.cumsum(padded)
    starts = ends - padded
    hot = ids[:, 0:2, None] == jnp.arange(N_EXPERTS, dtype=I32)[None, None, :]
    pos = (ids[:, 2:4] + jnp.sum(jnp.where(hot, starts[None, None, :], 0), axis=-1)).reshape(-1)
    n_used = (ends[-1] // EXPERT_TILE).astype(I32)
    tile_start = jnp.arange(n_tiles, dtype=I32) * EXPERT_TILE
    tile_e = jnp.minimum(jnp.sum((ends[None, :] <= tile_start[:, None]).astype(I32), axis=1),
                         N_EXPERTS - 1)
    last_e = jnp.sum(jnp.where(jnp.arange(n_tiles) == n_used - 1, tile_e, 0))
    tile_e = jnp.where(jnp.arange(n_tiles) < n_used, tile_e, last_e).astype(I32)
    return pos, starts + counts, padded - counts, tile_e, n_used.reshape(1)


def kernel(x, mem, mem_ln_g, mem_ln_b, w_in, attn_sinks, conv_dw, conv_dw_b, conv_ln_g, conv_ln_b,
           w_mem_kv, w_branch, w_out, ln1_g, ln1_b, router_group, router_group_b, router_expert,
           router_expert_b, w1, w3, w2, ln2_g, ln2_b):
    batch, seq, d = x.shape
    depth = w_in.shape[0]
    t = batch * seq
    n_slots = 2 * t + N_EXPERTS * EXPERT_TILE
    n_tiles = n_slots // EXPERT_TILE

    xf = x.reshape(t, d)
    xb = xf.astype(BF16)
    mem_n = _ln_rows(mem.reshape(batch * N_MEM, d), mem_ln_g, mem_ln_b, BF16, 512)

    for l in range(depth):
        proj = _project(xb, w_in, l, _in_proj_col_block, min(IN_TILE_M, t), IN_TILE_N, "in_proj")
        o_attn = _swa(proj, _swa_bias_mask(attn_sinks[l]), batch, seq)
        o_conv = _conv_branch(proj, conv_dw[l], conv_dw_b[l], conv_ln_g[l], conv_ln_b[l], batch, seq)
        kv = _project(mem_n, w_mem_kv, l, lambda j: j, 512, 512, "mem_kv")
        o_mem = _mem_attn(proj, kv, batch, seq)

        w_r = jnp.zeros((d, LANES), F32)
        w_r = w_r.at[:, :N_GROUPS].set(router_group[l]).at[:, N_GROUPS:N_GROUPS + N_EXPERTS].set(
            router_expert[l])
        b_r = jnp.zeros((1, LANES), F32)
        b_r = b_r.at[0, :N_GROUPS].set(router_group_b[l]).at[0, N_GROUPS:N_GROUPS + N_EXPERTS].set(
            router_expert_b[l])
        wr_hi = w_r.astype(BF16)
        wr_lo = (w_r - wr_hi.astype(F32)).astype(BF16)
        x1, x1p, ids, wts, counts = _merge(o_attn, o_conv, o_mem, proj, xf, w_branch[l].astype(BF16),
                                           w_out[l].astype(BF16), ln1_g[l], ln1_b[l],
                                           jnp.concatenate([wr_hi, wr_lo], axis=1), b_r)

        pos, pad_start, pad_len, tile_e, n_used = _dispatch_plan(ids, counts, n_tiles)
        xs = _dispatch(x1p, pos, pad_start, pad_len, n_used, n_slots)
        ys = _experts(xs, tile_e, n_used, w1, w3, w2, l)
        xf, xb = _combine(ys, pos, wts, x1, ln2_g[l], ln2_b[l])

    return xf.reshape(batch, seq, d)
```

```python
import jax
import jax.numpy as jnp
from jax import lax
from jax.experimental import pallas as pl
from jax.experimental.pallas import tpu as pltpu

F32 = jnp.float32
BF16 = jnp.bfloat16
U32 = jnp.uint32
I32 = jnp.int32

D_MODEL = 2048
BRANCH = 1024
HEAD_DIM = 64
N_Q_HEADS = 16
N_KV_HEADS = 4
GQA = 4
WINDOW = 128
BLOCK = 128
KV_WIDTH = N_KV_HEADS * HEAD_DIM
CONV_K = 31
CONV_HIST = 32
N_MEM = 256
MEM_HEADS = 4
MEM_HD = 256
N_GROUPS = 4
EPG = 8
N_EXPERTS = 32
D_EXPERT = 512
ALPHA = (2.0 * 4) ** 0.25
LN_EPS = 1e-5
NEG_INF = -1e30
LOG2E = 1.4426950408889634

LANES = 128
SUBLANES = 8

COL_Q, COL_A, COL_G, COL_QM, COL_GATES, COL_K, COL_V = 0, 1024, 2048, 3072, 4096, 10240, 10496
IN_TILE_N = 512
IN_TILE_M = 2048

EXPERT_TILE = 256
TOKEN_TILE = 256
VMEM_LIMIT = 56 * 1024 * 1024


def _cparams(sem, vmem=VMEM_LIMIT):
    return pltpu.CompilerParams(dimension_semantics=sem, vmem_limit_bytes=vmem)


def _layer_norm_rows(z, g, b):
    mu = jnp.mean(z, axis=-1, keepdims=True)
    c = z - mu
    var = jnp.mean(c * c, axis=-1, keepdims=True)
    return c * lax.rsqrt(var + LN_EPS) * g + b


def _pack_halves(x):
    n = x.shape[1] // 2
    lo = lax.bitcast_convert_type(x[:, :n].astype(BF16).astype(F32), U32)
    hi = lax.bitcast_convert_type(x[:, n:].astype(BF16).astype(F32), U32)
    return (lo >> 16) | hi


def _unpack_halves(u):
    lo = lax.bitcast_convert_type(u << 16, F32)
    hi = lax.bitcast_convert_type(u & jnp.uint32(0xFFFF0000), F32)
    return lo, hi


def _proj_kernel(x_ref, w_ref, o_ref):
    o_ref[...] = jnp.dot(x_ref[...], w_ref[...].astype(BF16),
                         preferred_element_type=F32).astype(o_ref.dtype)


def _project(x, w_all, layer, col_block, tm, tn, name):
    m, k = x.shape
    n = w_all.shape[2]
    return pl.pallas_call(
        _proj_kernel,
        grid=(m // tm, n // tn),
        in_specs=[pl.BlockSpec((tm, k), lambda i, j: (i, 0)),
                  pl.BlockSpec((None, k, tn), lambda i, j: (layer, 0, col_block(j)))],
        out_specs=pl.BlockSpec((tm, tn), lambda i, j: (i, j)),
        out_shape=jax.ShapeDtypeStruct((m, n), BF16),
        compiler_params=_cparams(("parallel", "parallel")),
        name=name,
    )(x, w_all)


def _in_proj_col_block(j):
    kv_src = 1024 // IN_TILE_N
    n_tiles = 10752 // IN_TILE_N
    return jnp.where(j < kv_src, j, jnp.where(j < n_tiles - 1, j + 1, kv_src))


def _ln_kernel(x_ref, g_ref, b_ref, o_ref):
    o_ref[...] = _layer_norm_rows(x_ref[...], g_ref[...], b_ref[...]).astype(o_ref.dtype)


def _ln_rows(x, g, b, out_dtype, tm):
    m, d = x.shape
    return pl.pallas_call(
        _ln_kernel,
        grid=(m // tm,),
        in_specs=[pl.BlockSpec((tm, d), lambda i: (i, 0)),
                  pl.BlockSpec((1, d), lambda i: (0, 0)),
                  pl.BlockSpec((1, d), lambda i: (0, 0))],
        out_specs=pl.BlockSpec((tm, d), lambda i: (i, 0)),
        out_shape=jax.ShapeDtypeStruct((m, d), out_dtype),
        compiler_params=_cparams(("parallel",)),
        name="mem_ln",
    )(x, g.reshape(1, d), b.reshape(1, d))


def _swap_halves(t):
    u = pltpu.bitcast(t, U32)
    return pltpu.bitcast(pltpu.roll(u, LANES // 2, axis=1), BF16)


def _swa_kernel(q_ref, kc_ref, kp_ref, vc_ref, vp_ref, bm_ref, o_ref):
    low = lax.broadcasted_iota(I32, (1, LANES), 1) < HEAD_DIM
    sink_row = lax.broadcasted_iota(I32, (2 * BLOCK, 1), 0) == 0
    zero = jnp.zeros((), BF16)
    scale = (HEAD_DIM ** -0.5) * LOG2E
    for kv in range(N_KV_HEADS):
        tile = slice((kv // 2) * LANES, (kv // 2 + 1) * LANES)
        kt = jnp.concatenate([kp_ref[:, tile], kc_ref[:, tile]], axis=0)
        vt = jnp.concatenate([vp_ref[:, tile], vc_ref[:, tile]], axis=0)
        kt = jnp.where(sink_row, zero, kt)
        vt = jnp.where(sink_row, zero, vt)
        ks, vs = _swap_halves(kt), _swap_halves(vt)
        k_lo, k_hi = (kt, ks) if kv % 2 == 0 else (ks, kt)
        v_lo, v_hi = (vt, vs) if kv % 2 == 0 else (vs, vt)
        kk = jnp.where(low, k_lo, k_hi)
        vv = (jnp.where(low, v_lo, zero), jnp.where(low, zero, v_hi))
        qts = [q_ref[:, (2 * kv + t) * LANES:(2 * kv + t + 1) * LANES] for t in range(2)]
        q4 = jnp.concatenate([jnp.where(low, qts[0], zero), jnp.where(low, zero, qts[0]),
                              jnp.where(low, qts[1], zero), jnp.where(low, zero, qts[1])], axis=0)
        s4 = lax.dot_general(q4, kk, (((1,), (1,)), ((), ())), preferred_element_type=F32)
        s4 = s4 * scale + bm_ref[0, kv]
        for t in range(2):
            acc = None
            for half in range(2):
                g = 2 * t + half
                s = s4[g * BLOCK:(g + 1) * BLOCK]
                p = jnp.exp2(s - jnp.max(s, axis=-1, keepdims=True))
                denom = jnp.sum(p, axis=-1, keepdims=True)
                o = jnp.dot(p.astype(BF16), vv[half], preferred_element_type=F32) / denom
                acc = o if acc is None else acc + o
            o_ref[:, (2 * kv + t) * LANES:(2 * kv + t + 1) * LANES] = acc.astype(o_ref.dtype)


def _swa_bias_mask(sinks):
    qi = jnp.arange(BLOCK)[:, None] + BLOCK
    si = jnp.arange(2 * BLOCK)[None, :]
    dist = qi - si
    valid = (dist >= 0) & (dist < WINDOW)
    hh = jnp.arange(1, N_Q_HEADS + 1, dtype=F32)
    slopes = jnp.exp2(-8.0 * hh / N_Q_HEADS)
    bias = -slopes[:, None, None] * dist.astype(F32)[None]
    later = jnp.where(valid[None], bias, NEG_INF)
    first = jnp.where((valid & (si >= BLOCK))[None], bias, NEG_INF)
    both = jnp.stack([first, later])
    both = jnp.where((si == 0)[None, None], sinks.astype(F32)[None, :, None, None], both) * LOG2E
    return both.reshape(2, N_KV_HEADS, GQA * BLOCK, 2 * BLOCK)


def _swa(proj, bias_mask, batch, seq):
    nb = seq // BLOCK
    kcol, vcol = COL_K // KV_WIDTH, COL_V // KV_WIDTH

    def cur(b, n):
        return b * nb + n

    def prev(b, n):
        return b * nb + jnp.maximum(n - 1, 0)

    return pl.pallas_call(
        _swa_kernel,
        grid=(batch, nb),
        in_specs=[pl.BlockSpec((BLOCK, BRANCH), lambda b, n: (cur(b, n), COL_Q // BRANCH)),
                  pl.BlockSpec((BLOCK, KV_WIDTH), lambda b, n: (cur(b, n), kcol)),
                  pl.BlockSpec((BLOCK, KV_WIDTH), lambda b, n: (prev(b, n), kcol)),
                  pl.BlockSpec((BLOCK, KV_WIDTH), lambda b, n: (cur(b, n), vcol)),
                  pl.BlockSpec((BLOCK, KV_WIDTH), lambda b, n: (prev(b, n), vcol)),
                  pl.BlockSpec((1, N_KV_HEADS, GQA * BLOCK, 2 * BLOCK),
                               lambda b, n: (jnp.minimum(n, 1), 0, 0, 0))],
        out_specs=pl.BlockSpec((BLOCK, BRANCH), lambda b, n: (cur(b, n), 0)),
        out_shape=jax.ShapeDtypeStruct((batch * seq, BRANCH), BF16),
        compiler_params=_cparams(("parallel", "arbitrary")),
        name="swa",
    )(proj, proj, proj, proj, proj, bias_mask)


CONV_TILE = 256
CONV_CHUNK = 64
CONV_ROWS = CONV_HIST + CONV_TILE


def _conv_kernel(ac_ref, gc_ref, ah_ref, gh_ref, w_ref, b_ref, lg_ref, lb_ref, o_ref, h_ref, c_ref):
    first = pl.program_id(1) == 0
    hist = ah_ref[...].astype(F32) * jax.nn.sigmoid(gh_ref[...].astype(F32))
    h_ref[0, 0:CONV_HIST, :] = jnp.where(first, 0.0, hist)
    h_ref[0, CONV_HIST:, :] = ac_ref[...].astype(F32) * jax.nn.sigmoid(gc_ref[...].astype(F32))
    span = CONV_ROWS - SUBLANES
    for s in range(1, SUBLANES):
        h_ref[s, 0:span, :] = h_ref[0, s:s + span, :]
    base = CONV_HIST - (CONV_K - 1)
    groups = CONV_CHUNK // SUBLANES
    for lt in range(BRANCH // LANES):
        cols = slice(lt * LANES, (lt + 1) * LANES)
        taps = [w_ref[j * SUBLANES:(j + 1) * SUBLANES, cols] for j in range(CONV_K)]
        bias = jnp.broadcast_to(b_ref[:, cols], (groups, SUBLANES, LANES))

        def chunk(i, carry, cols=cols, taps=taps, bias=bias):
            r0 = pl.multiple_of(i * CONV_CHUNK, CONV_CHUNK)
            acc = bias
            for s in range(SUBLANES):
                offs = [off for off in range(base, base + CONV_K) if off % SUBLANES == s]
                q0, q1 = offs[0] // SUBLANES, offs[-1] // SUBLANES
                rows = (q1 - q0) * SUBLANES + CONV_CHUNK
                win = h_ref[s, pl.ds(r0 + q0 * SUBLANES, rows), cols]
                win = win.reshape(rows // SUBLANES, SUBLANES, LANES)
                for off in offs:
                    q = off // SUBLANES - q0
                    acc = acc + taps[off - base][None] * win[q:q + groups]
            c_ref[pl.ds(r0, CONV_CHUNK), cols] = acc.reshape(CONV_CHUNK, LANES)
            return carry

        lax.fori_loop(0, CONV_TILE // CONV_CHUNK, chunk, 0)
    y = _layer_norm_rows(c_ref[...], lg_ref[...], lb_ref[...])
    o_ref[...] = (y * jax.nn.sigmoid(y)).astype(o_ref.dtype)


def _conv_branch(proj, w_dw, b_dw, ln_g, ln_b, batch, seq):
    nt = seq // CONV_TILE
    hist_per_tile = CONV_TILE // CONV_HIST
    w_rep = jnp.repeat(w_dw, SUBLANES, axis=0)

    def cur(b, n):
        return b * nt + n

    def hist(b, n):
        return jnp.maximum((b * nt + n) * hist_per_tile - 1, 0)

    vec = lambda: pl.BlockSpec((1, BRANCH), lambda b, n: (0, 0))
    return pl.pallas_call(
        _conv_kernel,
        grid=(batch, nt),
        in_specs=[pl.BlockSpec((CONV_TILE, BRANCH), lambda b, n: (cur(b, n), COL_A // BRANCH)),
                  pl.BlockSpec((CONV_TILE, BRANCH), lambda b, n: (cur(b, n), COL_G // BRANCH)),
                  pl.BlockSpec((CONV_HIST, BRANCH), lambda b, n: (hist(b, n), COL_A // BRANCH)),
                  pl.BlockSpec((CONV_HIST, BRANCH), lambda b, n: (hist(b, n), COL_G // BRANCH)),
                  pl.BlockSpec((CONV_K * SUBLANES, BRANCH), lambda b, n: (0, 0)),
                  vec(), vec(), vec()],
        out_specs=pl.BlockSpec((CONV_TILE, BRANCH), lambda b, n: (cur(b, n), 0)),
        out_shape=jax.ShapeDtypeStruct((batch * seq, BRANCH), BF16),
        scratch_shapes=[pltpu.VMEM((SUBLANES, CONV_ROWS, BRANCH), F32),
                        pltpu.VMEM((CONV_TILE, BRANCH), F32)],
        compiler_params=_cparams(("parallel", "arbitrary")),
        name="conv",
    )(proj, proj, proj, proj, w_rep, b_dw.reshape(1, BRANCH), ln_g.reshape(1, BRANCH),
      ln_b.reshape(1, BRANCH))


MEM_TILE = 512


def _mem_attn_kernel(q_ref, k_ref, v_ref, o_ref):
    for h in range(MEM_HEADS):
        sl = slice(h * MEM_HD, (h + 1) * MEM_HD)
        s = lax.dot_general(q_ref[:, sl], k_ref[:, sl], (((1,), (1,)), ((), ())),
                            preferred_element_type=F32) * (MEM_HD ** -0.5)
        m = jnp.max(s, axis=-1, keepdims=True)
        p = jnp.exp(s - m)
        denom = jnp.sum(p, axis=-1, keepdims=True)
        o = jnp.dot(p.astype(BF16), v_ref[:, sl], preferred_element_type=F32) / denom
        o_ref[:, sl] = o.astype(o_ref.dtype)


def _mem_attn(proj, kv, batch, seq):
    nt = seq // MEM_TILE
    return pl.pallas_call(
        _mem_attn_kernel,
        grid=(batch, nt),
        in_specs=[pl.BlockSpec((MEM_TILE, BRANCH), lambda b, n: (b * nt + n, COL_QM // BRANCH)),
                  pl.BlockSpec((N_MEM, BRANCH), lambda b, n: (b, 0)),
                  pl.BlockSpec((N_MEM, BRANCH), lambda b, n: (b, 1))],
        out_specs=pl.BlockSpec((MEM_TILE, BRANCH), lambda b, n: (b * nt + n, 0)),
        out_shape=jax.ShapeDtypeStruct((batch * seq, BRANCH), BF16),
        compiler_params=_cparams(("parallel", "parallel")),
        name="mem_attn",
    )(proj, kv, kv)


def _route(logits):
    lane_f = lax.broadcasted_iota(I32, logits.shape, 1).astype(F32)
    big = jnp.float32(1e9)
    low = jnp.float32(-3e38)
    gmask = lane_f < N_GROUPS
    gmax = jnp.max(jnp.where(gmask, logits, low), axis=-1, keepdims=True)
    grp = jnp.min(jnp.where(gmask & (logits == gmax), lane_f, big), axis=-1, keepdims=True)
    p_grp = 1.0 / jnp.sum(jnp.where(gmask, jnp.exp(logits - gmax), 0.0), axis=-1, keepdims=True)
    lo = N_GROUPS + grp * EPG
    emask = (lane_f >= lo) & (lane_f < lo + EPG)
    e1 = jnp.max(jnp.where(emask, logits, low), axis=-1, keepdims=True)
    i1 = jnp.min(jnp.where(emask & (logits == e1), lane_f, big), axis=-1, keepdims=True)
    rest = emask & (lane_f != i1)
    e2 = jnp.max(jnp.where(rest, logits, low), axis=-1, keepdims=True)
    i2 = jnp.min(jnp.where(rest & (logits == e2), lane_f, big), axis=-1, keepdims=True)
    t = jnp.exp(e2 - e1)
    w0 = p_grp / (1.0 + t)
    w1 = p_grp * t / (1.0 + t)
    return i1 - N_GROUPS, i2 - N_GROUPS, w0, w1


def _merge_kernel(oa_ref, oc_ref, om_ref, g0_ref, g1_ref, g2_ref, x_ref, wb_ref, wo_ref,
                  lg_ref, lb_ref, wr_ref, br_ref,
                  x1_ref, x1p_ref, ids_ref, wts_ref, cnt_ref, za_ref, zb_ref, run_ref):
    i = pl.program_id(0)

    @pl.when(i == 0)
    def _():
        zb_ref[...] = jnp.zeros_like(zb_ref)
        run_ref[...] = jnp.zeros_like(run_ref)

    refs = (oa_ref, oc_ref, om_ref, g0_ref, g1_ref, g2_ref, x_ref, wb_ref, wo_ref, lg_ref, lb_ref,
            wr_ref, br_ref, x1_ref, x1p_ref, ids_ref, wts_ref, cnt_ref, run_ref)

    @pl.when(i % 2 == 0)
    def _():
        _merge_step(i, za_ref, zb_ref, *refs)

    @pl.when(i % 2 == 1)
    def _():
        _merge_step(i, zb_ref, za_ref, *refs)


def _merge_step(i, zcur_ref, zprev_ref, oa_ref, oc_ref, om_ref, g0_ref, g1_ref, g2_ref, x_ref, wb_ref,
                wo_ref, lg_ref, lb_ref, wr_ref, br_ref, x1_ref, x1p_ref, ids_ref, wts_ref,
                cnt_ref, run_ref):
    tm = x_ref.shape[0]
    merged = jax.nn.sigmoid(g0_ref[...].astype(F32)) * jnp.dot(
        oa_ref[...], wb_ref[0], preferred_element_type=F32)
    merged = merged + jax.nn.sigmoid(g1_ref[...].astype(F32)) * jnp.dot(
        oc_ref[...], wb_ref[1], preferred_element_type=F32)
    merged = merged + jax.nn.sigmoid(g2_ref[...].astype(F32)) * jnp.dot(
        om_ref[...], wb_ref[2], preferred_element_type=F32)
    y = jnp.dot(merged.astype(BF16), wo_ref[...], preferred_element_type=F32)
    zcur_ref[...] = ALPHA * x_ref[...] + y

    x1 = _layer_norm_rows(zprev_ref[...], lg_ref[...], lb_ref[...])
    x1_ref[...] = x1
    x1p_ref[...] = _pack_halves(x1)
    r = jnp.dot(x1.astype(BF16), wr_ref[...], preferred_element_type=F32)
    logits = (r[:, :LANES] + r[:, LANES:]) + br_ref[...]
    e0, e1, w0, w1 = _route(logits)

    lane = lax.broadcasted_iota(I32, (tm, LANES), 1)
    lane_f = lane.astype(F32)
    hot0 = (lane_f == e0).astype(F32)
    hot1 = (lane_f == e1).astype(F32)
    both = hot0 + hot1
    tri = (lax.broadcasted_iota(I32, (tm, tm), 0) > lax.broadcasted_iota(I32, (tm, tm), 1)).astype(BF16)
    run = run_ref[...]
    before = jnp.dot(tri, both.astype(BF16), preferred_element_type=F32) + run
    r0 = jnp.sum(hot0 * before, axis=-1, keepdims=True)
    r1 = jnp.sum(hot1 * before, axis=-1, keepdims=True)
    run = run + jnp.where(i > 0, jnp.sum(both, axis=0, keepdims=True), 0.0)
    run_ref[...] = run
    cnt_ref[...] = run
    ids = jnp.where(lane == 0, e0, jnp.where(lane == 1, e1, jnp.where(lane == 2, r0, jnp.where(lane == 3, r1, 0.0))))
    ids_ref[...] = ids.astype(I32)
    wts_ref[...] = jnp.where(lane == 0, w0, jnp.where(lane == 1, w1, 0.0))


def _merge(o_attn, o_conv, o_mem, proj, x, w_branch, w_out, ln_g, ln_b, w_r, b_r):
    t = x.shape[0]
    tm = TOKEN_TILE
    nt = t // tm
    gcol = COL_GATES // D_MODEL
    cur = lambda w, c=0: pl.BlockSpec((tm, w), lambda i: (jnp.minimum(i, nt - 1), c))
    behind = lambda w: pl.BlockSpec((tm, w), lambda i: (jnp.maximum(i - 1, 0), 0))
    const = lambda shape: pl.BlockSpec(shape, lambda i: (0,) * len(shape), pipeline_mode=pl.Buffered(1))
    return pl.pallas_call(
        _merge_kernel,
        grid=(nt + 1,),
        in_specs=[cur(BRANCH), cur(BRANCH), cur(BRANCH),
                  cur(D_MODEL, gcol), cur(D_MODEL, gcol + 1), cur(D_MODEL, gcol + 2),
                  cur(D_MODEL),
                  const((3, BRANCH, D_MODEL)), const((D_MODEL, D_MODEL)),
                  const((1, D_MODEL)), const((1, D_MODEL)),
                  const((D_MODEL, 2 * LANES)), const((1, LANES))],
        out_specs=[behind(D_MODEL), behind(D_MODEL // 2), behind(LANES), behind(LANES),
                   pl.BlockSpec((1, LANES), lambda i: (0, 0))],
        out_shape=[jax.ShapeDtypeStruct((t, D_MODEL), F32),
                   jax.ShapeDtypeStruct((t, D_MODEL // 2), U32),
                   jax.ShapeDtypeStruct((t, LANES), I32),
                   jax.ShapeDtypeStruct((t, LANES), F32),
                   jax.ShapeDtypeStruct((1, LANES), F32)],
        scratch_shapes=[pltpu.VMEM((tm, D_MODEL), F32), pltpu.VMEM((tm, D_MODEL), F32),
                        pltpu.VMEM((1, LANES), F32)],
        compiler_params=_cparams(("arbitrary",)),
        name="merge",
    )(o_attn, o_conv, o_mem, proj, proj, proj, x, w_branch, w_out,
      ln_g.reshape(1, D_MODEL), ln_b.reshape(1, D_MODEL), w_r, b_r)


def _row_copy(src, src_row, dst, dst_row, sem):
    return pltpu.make_async_copy(src.at[pl.ds(src_row, 1), :], dst.at[pl.ds(dst_row, 1), :], sem)


def _dispatch_kernel(pad_start_ref, pad_len_ref, nu_ref, pos_ref, x_ref, xs_ref, z_ref, sem, zsem):
    tm = x_ref.shape[0]

    def start(r, carry):
        for k in range(2):
            _row_copy(x_ref, r, xs_ref, pos_ref[0, 0, 2 * r + k], sem).start(priority=k)
        return carry

    lax.fori_loop(0, tm, start, 0, unroll=8)

    @pl.when(pl.program_id(0) == pl.num_programs(0) - 1)
    def _():
        z_ref[...] = jnp.zeros_like(z_ref)

        def pad_copies(e, act):
            base = pad_start_ref[e]
            n_pad = pad_len_ref[e]
            off = base + n_pad
            for bit in range(EXPERT_TILE.bit_length() - 2, SUBLANES.bit_length() - 2, -1):
                size = 1 << bit
                has = (n_pad >> bit) & 1
                off = off - has * size

                @pl.when(has == 1)
                def _(off=off, size=size):
                    row = pl.multiple_of(off, size)
                    act(pltpu.make_async_copy(z_ref.at[pl.ds(0, size), :],
                                              xs_ref.at[pl.ds(row, size), :], zsem))

            def low_rows(r, c):
                act(_row_copy(z_ref, 0, xs_ref, base + r, zsem))
                return c

            lax.fori_loop(0, n_pad & (SUBLANES - 1), low_rows, 0)

        def start_all(e, carry):
            pad_copies(e, lambda cp: cp.start())
            return carry

        def wait_all(e, carry):
            pad_copies(e, lambda cp: cp.wait())
            return carry

        lax.fori_loop(0, N_EXPERTS, start_all, 0)
        lax.fori_loop(0, N_EXPERTS, wait_all, 0)

        def tile_copy(tile):
            row = pl.multiple_of(tile * EXPERT_TILE, EXPERT_TILE)
            return pltpu.make_async_copy(z_ref, xs_ref.at[pl.ds(row, EXPERT_TILE), :], zsem)

        def tstart(tile, c):
            tile_copy(tile).start()
            return c

        def twait(tile, c):
            tile_copy(tile).wait()
            return c

        n_tiles = xs_ref.shape[0] // EXPERT_TILE
        lax.fori_loop(nu_ref[0], n_tiles, tstart, 0)
        lax.fori_loop(nu_ref[0], n_tiles, twait, 0)

    for _ in range(2 * tm):
        _row_copy(x_ref, 0, xs_ref, 0, sem).wait()


def _dispatch(x1p, pos, pad_start, pad_len, n_used, n_slots):
    t, w = x1p.shape
    tm = TOKEN_TILE
    grid_spec = pltpu.PrefetchScalarGridSpec(
        num_scalar_prefetch=3,
        grid=(t // tm,),
        in_specs=[pl.BlockSpec((1, 1, 2 * tm), lambda i, ps, pn, nu: (i, 0, 0), memory_space=pltpu.SMEM),
                  pl.BlockSpec((tm, w), lambda i, ps, pn, nu: (i, 0))],
        out_specs=pl.BlockSpec(memory_space=pl.ANY),
        scratch_shapes=[pltpu.VMEM((EXPERT_TILE, w), U32), pltpu.SemaphoreType.DMA(()),
                        pltpu.SemaphoreType.DMA(())],
    )
    return pl.pallas_call(
        _dispatch_kernel,
        grid_spec=grid_spec,
        out_shape=jax.ShapeDtypeStruct((n_slots, w), U32),
        compiler_params=pltpu.CompilerParams(dimension_semantics=("arbitrary",),
                                             vmem_limit_bytes=VMEM_LIMIT, has_side_effects=True),
        name="dispatch",
    )(pad_start, pad_len, n_used, pos.reshape(t // tm, 1, 2 * tm), x1p)


def _expert_kernel(te_ref, nu_ref, xs_ref, w1_ref, w3_ref, w2_ref, ys_ref, w1b_ref, w3b_ref, w2b_ref):
    i = pl.program_id(0)
    used = i < nu_ref[0]
    new_expert = (i == 0) | (te_ref[i] != te_ref[jnp.maximum(i - 1, 0)])

    @pl.when(new_expert)
    def _():
        w1b_ref[...] = w1_ref[...].astype(BF16)
        w3b_ref[...] = w3_ref[...].astype(BF16)
        w2b_ref[...] = w2_ref[...].astype(BF16)

    @pl.when(used)
    def _():
        half = D_MODEL // 2
        lo, hi = _unpack_halves(xs_ref[...])
        lo = lo.astype(BF16)
        hi = hi.astype(BF16)
        a = (jnp.dot(lo, w1b_ref[:half, :], preferred_element_type=F32)
             + jnp.dot(hi, w1b_ref[half:, :], preferred_element_type=F32))
        b = (jnp.dot(lo, w3b_ref[:half, :], preferred_element_type=F32)
             + jnp.dot(hi, w3b_ref[half:, :], preferred_element_type=F32))
        h = (a * jax.nn.sigmoid(a)) * b
        y = jnp.dot(h.astype(BF16), w2b_ref[...], preferred_element_type=F32)
        ys_ref[...] = _pack_halves(y)

    @pl.when(jnp.logical_not(used))
    def _():
        ys_ref[...] = jnp.zeros_like(ys_ref)


def _experts(xs, tile_expert, n_used, w1, w3, w2, layer):
    n_slots, w = xs.shape
    te = EXPERT_TILE
    wspec = lambda a, b: pl.BlockSpec((None, None, a, b), lambda i, te_r, nu_r: (layer, te_r[i], 0, 0))
    grid_spec = pltpu.PrefetchScalarGridSpec(
        num_scalar_prefetch=2,
        grid=(n_slots // te,),
        in_specs=[pl.BlockSpec((te, w), lambda i, te_r, nu_r: (jnp.minimum(i, nu_r[0] - 1), 0)),
                  wspec(D_MODEL, D_EXPERT), wspec(D_MODEL, D_EXPERT), wspec(D_EXPERT, D_MODEL)],
        out_specs=pl.BlockSpec((te, w), lambda i, te_r, nu_r: (i, 0)),
        scratch_shapes=[pltpu.VMEM((D_MODEL, D_EXPERT), BF16), pltpu.VMEM((D_MODEL, D_EXPERT), BF16),
                        pltpu.VMEM((D_EXPERT, D_MODEL), BF16)],
    )
    return pl.pallas_call(
        _expert_kernel,
        grid_spec=grid_spec,
        out_shape=jax.ShapeDtypeStruct((n_slots, w), U32),
        compiler_params=_cparams(("arbitrary",)),
        name="experts",
    )(tile_expert, n_used, xs, w1, w3, w2)


def _combine_kernel(pos_ref, posn_ref, wts_ref, x1_ref, lg_ref, lb_ref, ys_ref,
                    x2_ref, x2b_ref, bufa_ref, bufb_ref, sema, semb):
    i = pl.program_id(0)
    n = pl.num_programs(0)
    tm = TOKEN_TILE

    def start(p_ref, base, buf, sem, r):
        for k in range(2):
            _row_copy(ys_ref, p_ref[0, 0, base + 2 * r + k], buf.at[k], r, sem).start(priority=k)

    def drain(buf, sem):
        for _ in range(2 * tm):
            _row_copy(ys_ref, 0, buf.at[0], 0, sem).wait()

    def finish(buf, rows):
        lo0, hi0 = _unpack_halves(buf[0])
        lo1, hi1 = _unpack_halves(buf[1])
        w0 = wts_ref[rows, 0:1]
        w1 = wts_ref[rows, 1:2]
        f = jnp.concatenate([w0 * lo0 + w1 * lo1, w0 * hi0 + w1 * hi1], axis=1)
        x2 = _layer_norm_rows(ALPHA * x1_ref[rows, :] + f, lg_ref[...], lb_ref[...])
        x2_ref[rows, :] = x2
        x2b_ref[rows, :] = x2.astype(BF16)

    @pl.when(i == 0)
    def _():
        def body(r, carry):
            start(pos_ref, 0, bufa_ref, sema, r)
            return carry
        lax.fori_loop(0, tm, body, 0, unroll=8)

    drain(bufa_ref, sema)
    for r in range(tm):
        start(pos_ref, 2 * tm, bufb_ref, semb, r)
    finish(bufa_ref, slice(0, tm))
    drain(bufb_ref, semb)
    for r in range(tm):
        start(posn_ref, 0, bufa_ref, sema, r)
    finish(bufb_ref, slice(tm, 2 * tm))

    @pl.when(i == n - 1)
    def _():
        drain(bufa_ref, sema)


def _combine(ys, pos, wts, x1, ln_g, ln_b):
    t = x1.shape[0]
    tm = TOKEN_TILE
    ns = t // (2 * tm)
    w = ys.shape[1]
    pos3 = pos.reshape(ns, 1, 4 * tm)
    row = lambda width: pl.BlockSpec((2 * tm, width), lambda i: (i, 0))
    vec = lambda: pl.BlockSpec((1, D_MODEL), lambda i: (0, 0))
    return pl.pallas_call(
        _combine_kernel,
        grid=(ns,),
        in_specs=[pl.BlockSpec((1, 1, 4 * tm), lambda i: (i, 0, 0), memory_space=pltpu.SMEM),
                  pl.BlockSpec((1, 1, 4 * tm), lambda i: (jnp.minimum(i + 1, ns - 1), 0, 0),
                               memory_space=pltpu.SMEM),
                  row(LANES), row(D_MODEL), vec(), vec(),
                  pl.BlockSpec(memory_space=pl.ANY)],
        out_specs=[row(D_MODEL), row(D_MODEL)],
        out_shape=[jax.ShapeDtypeStruct((t, D_MODEL), F32),
                   jax.ShapeDtypeStruct((t, D_MODEL), BF16)],
        scratch_shapes=[pltpu.VMEM((2, tm, w), U32), pltpu.VMEM((2, tm, w), U32),
                        pltpu.SemaphoreType.DMA(()), pltpu.SemaphoreType.DMA(())],
        compiler_params=_cparams(("arbitrary",)),
        name="combine",
    )(pos3, pos3, wts, x1, ln_g.reshape(1, D_MODEL), ln_b.reshape(1, D_MODEL), ys)


def _dispatch_plan(ids, counts_f, n_tiles):
    counts = counts_f[0, :N_EXPERTS].astype(I32)
    padded = ((counts + EXPERT_TILE - 1) // EXPERT_TILE) * EXPERT_TILE
    ends = jnp.cumsum(padded)
    starts = ends - padded
    hot = ids[:, 0:2, None] == jnp.arange(N_EXPERTS, dtype=I32)[None, None, :]
    pos = (ids[:, 2:4] + jnp.sum(jnp.where(hot, starts[None, None, :], 0), axis=-1)).reshape(-1)
    n_used = (ends[-1] // EXPERT_TILE).astype(I32)
    tile_start = jnp.arange(n_tiles, dtype=I32) * EXPERT_TILE
    tile_e = jnp.minimum(jnp.sum((ends[None, :] <= tile_start[:, None]).astype(I32), axis=1),
                         N_EXPERTS - 1)
    last_e = jnp.sum(jnp.where(jnp.arange(n_tiles) == n_used - 1, tile_e, 0))
    tile_e = jnp.where(jnp.arange(n_tiles) < n_used, tile_e, last_e).astype(I32)
    return pos, starts + counts, padded - counts, tile_e, n_used.reshape(1)


def kernel(x, mem, mem_ln_g, mem_ln_b, w_in, attn_sinks, conv_dw, conv_dw_b, conv_ln_g, conv_ln_b,
           w_mem_kv, w_branch, w_out, ln1_g, ln1_b, router_group, router_group_b, router_expert,
           router_expert_b, w1, w3, w2, ln2_g, ln2_b):
    batch, seq, d = x.shape
    depth = w_in.shape[0]
    t = batch * seq
    n_slots = 2 * t + N_EXPERTS * EXPERT_TILE
    n_tiles = n_slots // EXPERT_TILE

    xf = x.reshape(t, d)
    xb = xf.astype(BF16)
    mem_n = _ln_rows(mem.reshape(batch * N_MEM, d), mem_ln_g, mem_ln_b, BF16, 512)

    for l in range(depth):
        proj = _project(xb, w_in, l, _in_proj_col_block, min(IN_TILE_M, t), IN_TILE_N, "in_proj")
        o_attn = _swa(proj, _swa_bias_mask(attn_sinks[l]), batch, seq)
        o_conv = _conv_branch(proj, conv_dw[l], conv_dw_b[l], conv_ln_g[l], conv_ln_b[l], batch, seq)
        kv = _project(mem_n, w_mem_kv, l, lambda j: j, 512, 512, "mem_kv")
        o_mem = _mem_attn(proj, kv, batch, seq)

        w_r = jnp.zeros((d, LANES), F32)
        w_r = w_r.at[:, :N_GROUPS].set(router_group[l]).at[:, N_GROUPS:N_GROUPS + N_EXPERTS].set(
            router_expert[l])
        b_r = jnp.zeros((1, LANES), F32)
        b_r = b_r.at[0, :N_GROUPS].set(router_group_b[l]).at[0, N_GROUPS:N_GROUPS + N_EXPERTS].set(
            router_expert_b[l])
        wr_hi = w_r.astype(BF16)
        wr_lo = (w_r - wr_hi.astype(F32)).astype(BF16)
        x1, x1p, ids, wts, counts = _merge(o_attn, o_conv, o_mem, proj, xf, w_branch[l].astype(BF16),
                                           w_out[l].astype(BF16), ln1_g[l], ln1_b[l],
                                           jnp.concatenate([wr_hi, wr_lo], axis=1), b_r)

        pos, pad_start, pad_len, tile_e, n_used = _dispatch_plan(ids, counts, n_tiles)
        xs = _dispatch(x1p, pos, pad_start, pad_len, n_used, n_slots)
        ys = _experts(xs, tile_e, n_used, w1, w3, w2, l)
        xf, xb = _combine(ys, pos, wts, x1, ln2_g[l], ln2_b[l])

    return xf.reshape(batch, seq, d)
```
